```python
import jax
import jax.numpy as jnp
from jax import lax
import numpy as np

D_MODEL = 1024
BATCH = 8
SEQ = 4096
DEPTH = 2

N_BRANCHES = 4
BRANCH_WIDTH = D_MODEL // N_BRANCHES
LRU_CONV = 4
LRU_BLOCKS = 8
LRU_BLOCK = BRANCH_WIDTH // LRU_BLOCKS
LRU_C = 8.0
POOL_WINDOWS = (2, 4, 8, 16)
POOL_GROUP = BRANCH_WIDTH // len(POOL_WINDOWS)
HGRN_HEADS = 4
HGRN_DK = BRANCH_WIDTH // HGRN_HEADS
HGRN_DV = BRANCH_WIDTH // HGRN_HEADS
HGRN_CHUNK = 64
SCONV_WIDTH = 3
D_FF = 4 * D_MODEL
N_MIX_SLOTS = 10
IN_WIDTH = N_MIX_SLOTS * BRANCH_WIDTH + N_BRANCHES * D_MODEL
EPS = 1e-6

kernel_name = "hybrid_gated_mixer_trunk"


def rms_norm(x, gain):
    xf = x.astype(jnp.float32)
    y = xf * lax.rsqrt(jnp.mean(xf * xf, axis=-1, keepdims=True) + EPS)
    return (y * gain.astype(jnp.float32)).astype(x.dtype)


def causal_depthwise_conv(x, w):
    k = w.shape[0]
    c = x.shape[-1]
    return lax.conv_general_dilated(
        x, w.astype(x.dtype).reshape(k, 1, c), window_strides=(1,),
        padding=[(k - 1, 0)], dimension_numbers=("NWC", "WIO", "NWC"),
        feature_group_count=c)


def block_diag_linear(x, w, b):
    bsz, t, _ = x.shape
    nb, bi, bo = w.shape
    y = jnp.einsum("btnc,ncd->btnd", x.reshape(bsz, t, nb, bi), w.astype(x.dtype))
    return y.reshape(bsz, t, nb * bo) + b.astype(x.dtype)


def rg_lru_branch(x_in, gate_in, conv_w, conv_b, w_a, b_a, w_x, b_x, lam):
    xc = causal_depthwise_conv(x_in, conv_w) + conv_b.astype(x_in.dtype)
    r = jax.nn.sigmoid(block_diag_linear(xc, w_a, b_a).astype(jnp.float32))
    i = jax.nn.sigmoid(block_diag_linear(xc, w_x, b_x).astype(jnp.float32))
    log_a = -LRU_C * r * jax.nn.softplus(-lam.astype(jnp.float32))
    a = jnp.exp(log_a)
    mult = jnp.sqrt(-jnp.expm1(2.0 * log_a))
    u = mult * i * xc.astype(jnp.float32)

    def combine(left, right):
        a_l, h_l = left
        a_r, h_r = right
        return a_l * a_r, a_r * h_l + h_r

    _, h = lax.associative_scan(combine, (a, u), axis=1)
    return (h * jax.nn.gelu(gate_in.astype(jnp.float32))).astype(x_in.dtype)


def pool_branch(x, w, scale):
    bsz, t, _ = x.shape
    xf = x.astype(jnp.float32)
    cs0 = jnp.pad(jnp.cumsum(xf, axis=1), ((0, 0), (1, 0), (0, 0)))
    pos = jnp.arange(1, t + 1, dtype=jnp.float32)[:, None]
    means = []
    for g, win in enumerate(POOL_WINDOWS):
        c = cs0[..., g * POOL_GROUP:(g + 1) * POOL_GROUP]
        window_sum = c[:, 1:] - jnp.pad(c, ((0, 0), (win - 1, 0), (0, 0)))[:, :t]
        means.append(window_sum / jnp.minimum(pos, float(win)))
    d = (jnp.concatenate(means, axis=-1) - xf).astype(x.dtype)
    d = d.reshape(bsz, t, len(POOL_WINDOWS), POOL_GROUP)
    y = jnp.einsum("btgc,gcd->btgd", d, w.astype(x.dtype)).reshape(bsz, t, BRANCH_WIDTH)
    return y * scale.astype(x.dtype)


def hgrn2_chunked(q, log_f, k, v):
    bsz, t, h, dk = q.shape
    dv = v.shape[-1]
    nc = t // HGRN_CHUNK

    def chunks(z):
        return z.reshape(bsz, nc, HGRN_CHUNK, h, z.shape[-1]).transpose(1, 0, 3, 2, 4)

    causal = jnp.tril(jnp.ones((HGRN_CHUNK, HGRN_CHUNK), dtype=bool))[:, :, None]

    def step(state, inp):
        q_c, lf_c, k_c, v_c = inp
        b = jnp.cumsum(lf_c, axis=2)
        diff = b[:, :, :, None, :] - b[:, :, None, :, :]
        decay = jnp.exp(jnp.where(causal, diff, -jnp.inf))
        scores = jnp.einsum("bhtk,bhsk,bhtsk->bhts", q_c, k_c, decay)
        o = (jnp.einsum("bhts,bhsv->bhtv", scores, v_c)
             + jnp.einsum("bhtk,bhkv->bhtv", q_c * jnp.exp(b), state))
        b_last = b[:, :, -1:, :]
        state = (jnp.exp(b_last[:, :, 0, :, None]) * state
                 + jnp.einsum("bhsk,bhsv->bhkv", k_c * jnp.exp(b_last - b), v_c))
        return state, o

    s0 = jnp.zeros((bsz, h, dk, dv), jnp.float32)
    _, o = lax.scan(step, s0, (chunks(q), chunks(log_f), chunks(k), chunks(v)))
    return o.transpose(1, 0, 3, 2, 4).reshape(bsz, t, h, dv)


def hgrn2_branch(hq, hf, hi, hg, lb, norm_gain):
    bsz, t, _ = hq.shape
    shp_k = (bsz, t, HGRN_HEADS, HGRN_DK)
    lb = lb.reshape(HGRN_HEADS, HGRN_DK)
    f = lb + (1.0 - lb) * jax.nn.sigmoid(hf.astype(jnp.float32).reshape(shp_k))
    o = hgrn2_chunked(hq.astype(jnp.float32).reshape(shp_k), jnp.log(f), 1.0 - f,
                      hi.astype(jnp.float32).reshape(bsz, t, HGRN_HEADS, HGRN_DV))
    o = rms_norm(o, norm_gain).reshape(bsz, t, BRANCH_WIDTH)
    return (o * jax.nn.silu(hg.astype(jnp.float32))).astype(hq.dtype)


def short_conv_branch(gate_b, gate_c, xs, w):
    return gate_b * causal_depthwise_conv(gate_c * xs, w)


def setup_inputs(seed: int = 0) -> dict:
    key = jax.random.key(seed)
    ks = jax.random.split(key, 24)
    f32 = jnp.float32
    W = BRANCH_WIDTH

    def nrm(k, shape, scale):
        return jax.random.normal(k, shape, f32) * scale

    def gain(k, shape):
        return 1.0 + 0.05 * jax.random.normal(k, shape, f32)

    u = jax.random.uniform(ks[12], (DEPTH, W), f32, minval=0.9, maxval=0.999)
    s = u ** (1.0 / LRU_C)
    lam = jnp.log(s) - jnp.log1p(-s)
    return {
        "x": jax.random.normal(ks[0], (BATCH, SEQ, D_MODEL), f32),
        "norm_mix_pre": gain(ks[1], (DEPTH, D_MODEL)),
        "norm_mix_post": gain(ks[2], (DEPTH, D_MODEL)),
        "norm_mlp_pre": gain(ks[3], (DEPTH, D_MODEL)),
        "norm_mlp_post": gain(ks[4], (DEPTH, D_MODEL)),
        "w_in": nrm(ks[5], (DEPTH, D_MODEL, IN_WIDTH), D_MODEL ** -0.5),
        "lru_conv_w": nrm(ks[6], (DEPTH, LRU_CONV, W), LRU_CONV ** -0.5),
        "lru_conv_b": nrm(ks[7], (DEPTH, W), 0.02),
        "lru_w_a": nrm(ks[8], (DEPTH, LRU_BLOCKS, LRU_BLOCK, LRU_BLOCK), LRU_BLOCK ** -0.5),
        "lru_b_a": nrm(ks[9], (DEPTH, W), 0.02),
        "lru_w_x": nrm(ks[10], (DEPTH, LRU_BLOCKS, LRU_BLOCK, LRU_BLOCK), LRU_BLOCK ** -0.5),
        "lru_b_x": nrm(ks[11], (DEPTH, W), 0.02),
        "lru_lambda": lam,
        "pool_w": nrm(ks[13], (DEPTH, len(POOL_WINDOWS), POOL_GROUP, POOL_GROUP), POOL_GROUP ** -0.5),
        "pool_scale": 1.0 + 0.1 * jax.random.normal(ks[14], (DEPTH, W), f32),
        "hgrn_lower_bound": nrm(ks[15], (DEPTH, W), 0.5),
        "hgrn_norm": gain(ks[16], (DEPTH, HGRN_DV)),
        "sconv_w": nrm(ks[17], (DEPTH, SCONV_WIDTH, W), SCONV_WIDTH ** -0.5),
        "w_branch": nrm(ks[18], (DEPTH, N_BRANCHES, W, D_MODEL), W ** -0.5),
        "w_out": nrm(ks[19], (DEPTH, D_MODEL, D_MODEL), D_MODEL ** -0.5),
        "w_up": nrm(ks[20], (DEPTH, D_MODEL, D_FF), D_MODEL ** -0.5),
        "w_down": nrm(ks[21], (DEPTH, D_FF, D_MODEL), D_FF ** -0.5),
    }


def reference(x, norm_mix_pre, norm_mix_post, norm_mlp_pre, norm_mlp_post, w_in,
              lru_conv_w, lru_conv_b, lru_w_a, lru_b_a, lru_w_x, lru_b_x, lru_lambda,
              pool_w, pool_scale, hgrn_lower_bound, hgrn_norm, sconv_w,
              w_branch, w_out, w_up, w_down):
    bsz, t, _ = x.shape
    lb_cum = jnp.cumsum(jax.nn.softmax(hgrn_lower_bound.astype(jnp.float32), axis=0), axis=0)
    lower_bounds = lb_cum - lb_cum[0:1]
    split_points = [BRANCH_WIDTH * (j + 1) for j in range(N_MIX_SLOTS)]
    h = x
    for l in range(DEPTH):
        u = rms_norm(h, norm_mix_pre[l])
        proj = jnp.einsum("btd,dp->btp", u, w_in[l])
        (a_x, a_gate, p_x, c_q, c_f, c_i, c_g, s_b, s_c, s_x,
         gate_logits) = jnp.split(proj, split_points, axis=-1)
        y_a = rg_lru_branch(a_x, a_gate, lru_conv_w[l], lru_conv_b[l], lru_w_a[l], lru_b_a[l],
                            lru_w_x[l], lru_b_x[l], lru_lambda[l])
        y_b = pool_branch(p_x, pool_w[l], pool_scale[l])
        y_c = hgrn2_branch(c_q, c_f, c_i, c_g, lower_bounds[l], hgrn_norm[l])
        y_d = short_conv_branch(s_b, s_c, s_x, sconv_w[l])
        ys = jnp.stack([y_a, y_b, y_c, y_d], axis=2)
        branch_out = jnp.einsum("btkc,kcd->btkd", ys, w_branch[l])
        gates = jax.nn.sigmoid(gate_logits.reshape(bsz, t, N_BRANCHES, D_MODEL))
        merged = jnp.sum(gates * branch_out, axis=2)
        mix = jnp.einsum("btd,de->bte", merged, w_out[l])
        h = h + rms_norm(mix, norm_mix_post[l])
        u = rms_norm(h, norm_mlp_pre[l])
        hid = jnp.square(jax.nn.relu(jnp.einsum("btd,df->btf", u, w_up[l])))
        m = jnp.einsum("btf,fd->btd", hid, w_down[l])
        h = h + rms_norm(m, norm_mlp_post[l])
    return h
```

```python
import functools

import jax
import jax.numpy as jnp
from jax import lax
from jax.experimental import pallas as pl
from jax.experimental.pallas import tpu as pltpu

D_MODEL = 1024
N_BRANCHES = 4
W = D_MODEL // N_BRANCHES
LRU_CONV = 4
LRU_C = 8.0
POOL_WINDOWS = (2, 4, 8, 16)
POOL_GROUP = W // len(POOL_WINDOWS)
HGRN_HEADS = 4
HGRN_DK = W // HGRN_HEADS
SCONV_WIDTH = 3
D_FF = 4 * D_MODEL
N_MIX_SLOTS = 10
MIX_WIDTH = N_MIX_SLOTS * W
EPS = 1e-6

TIME_TILE = 512
HGRN_CHUNK = 64
HGRN_DIAG = 8
TAIL = 16
MLP_TILE = 512
FF_CHUNK = 1024
VMEM_LIMIT_BYTES = 56 * 1024 * 1024

F32 = jnp.float32
BF16 = jnp.bfloat16


def _dot(a, b):
    return jnp.dot(a, b, preferred_element_type=F32)


def _dot_nt(a, b):
    return lax.dot_general(a, b, (((1,), (1,)), ((), ())), preferred_element_type=F32)


def _dot_tn(a, b):
    return lax.dot_general(a, b, (((0,), (0,)), ((), ())), preferred_element_type=F32)


def _rms(x, gain):
    return x * lax.rsqrt(jnp.mean(x * x, axis=-1, keepdims=True) + EPS) * gain


def _shift_rows(ext, s, n):
    if s == 0:
        return ext[TAIL:TAIL + n]
    return pltpu.roll(ext, s, 0)[TAIL:TAIL + n]


def _mixer_kernel(lb_layer, h_ref, gpre_ref, gpost_ref, win_ref, convw_ref, convb_ref, wa_ref, ba_ref,
                  wx_ref, bx_ref, lam_ref, poolw_ref, pools_ref, lbraw_ref, hnorm_ref, sconvw_ref,
                  wbr_ref, wout_ref, out_ref, tail_ref, hlru_ref, st_ref):
    tt = h_ref.shape[1]
    n_chunks = tt // HGRN_CHUNK

    @pl.when(pl.program_id(1) == 0)
    def _():
        tail_ref[...] = jnp.zeros_like(tail_ref)
        hlru_ref[...] = jnp.zeros_like(hlru_ref)
        st_ref[...] = jnp.zeros_like(st_ref)

    h_in = h_ref[0]
    u = _rms(h_in, gpre_ref[...]).astype(BF16)

    def proj(slot):
        return _dot(u, win_ref[:, slot * W:(slot + 1) * W])

    row = lax.broadcasted_iota(jnp.int32, (tt, W), 0)
    lane = lax.broadcasted_iota(jnp.int32, (tt, W), 1)

    a_x = proj(0)
    ext = jnp.concatenate([tail_ref[0], a_x], axis=0)
    tail_ref[0] = a_x[tt - TAIL:]
    xc = convb_ref[...] + sum(convw_ref[k:k + 1, :] * _shift_rows(ext, LRU_CONV - 1 - k, tt)
                              for k in range(LRU_CONV))
    xc_b = xc.astype(BF16)
    r = jax.nn.sigmoid(_dot(xc_b, wa_ref[...]) + ba_ref[...])
    i_gate = jax.nn.sigmoid(_dot(xc_b, wx_ref[...]) + bx_ref[...])
    log_a = (-LRU_C) * r * jax.nn.softplus(-lam_ref[...])
    a = jnp.exp(log_a)
    mult = jnp.sqrt(jnp.tanh(-log_a) * (1.0 + a * a))
    uu = mult * i_gate * xc
    s = 1
    while s < tt:
        keep = row >= s
        a_sh = jnp.where(keep, pltpu.roll(a, s, 0), 1.0)
        u_sh = jnp.where(keep, pltpu.roll(uu, s, 0), 0.0)
        uu = uu + a * u_sh
        a = a * a_sh
        s *= 2
    h_seq = uu + a * hlru_ref[0:1, :]
    hlru_ref[0:1, :] = h_seq[tt - 1:tt, :]
    y_a = h_seq * jax.nn.gelu(proj(1))

    p_x = proj(2)
    ext = jnp.concatenate([tail_ref[1], p_x], axis=0)
    tail_ref[1] = p_x[tt - TAIL:]
    group = lax.shift_right_logical(lane, POOL_GROUP.bit_length() - 1)
    win_sum = ext
    mean = jnp.zeros((tt, W), F32)
    pos = (pl.program_id(1) * tt + row + 1).astype(F32)
    for g, win in enumerate(POOL_WINDOWS):
        win_sum = win_sum + pltpu.roll(win_sum, win // 2, 0)
        mean = jnp.where(group == g, win_sum[TAIL:] / jnp.minimum(pos, float(win)), mean)
    y_b = _dot((mean - p_x).astype(BF16), poolw_ref[...]) * pools_ref[...]

    c_q = proj(3)
    c_f = proj(4)
    v = proj(5)
    lb_sm = jax.nn.softmax(lbraw_ref[...], axis=0)
    lb = jnp.sum(lb_sm[1:lb_layer + 1], axis=0, keepdims=True) if lb_layer else jnp.zeros((1, W), F32)
    f = lb + (1.0 - lb) * jax.nn.sigmoid(c_f)
    lf = jnp.log(f)
    kk = 1.0 - f
    rowc = row & (HGRN_CHUNK - 1)
    b = lf
    s = 1
    while s < HGRN_CHUNK:
        b = b + jnp.where(rowc >= s, pltpu.roll(b, s, 0), 0.0)
        s *= 2
    b3 = b.reshape(n_chunks, HGRN_CHUNK, W)
    b_last = b3[:, HGRN_CHUNK - 1:HGRN_CHUNK, :]
    qe_b = (c_q * jnp.exp(b)).astype(BF16)
    kd_b = (kk.reshape(n_chunks, HGRN_CHUNK, W) * jnp.exp(b_last - b3)).reshape(tt, W).astype(BF16)
    decay_chunk = jnp.exp(b_last)
    v_b = v.astype(BF16)

    sq_r = lax.broadcasted_iota(jnp.int32, (W, W), 0)
    sq_c = lax.broadcasted_iota(jnp.int32, (W, W), 1)
    same_head = (sq_r ^ sq_c) < HGRN_DK
    head_ones = jnp.where(same_head, 1.0, 0.0).astype(BF16)

    levels = []
    half = HGRN_DIAG
    while half < HGRN_CHUNK:
        blk = 2 * half
        bb = b.reshape(tt // blk, blk, W)
        delta = (bb - bb[:, half - 1:half, :]).reshape(tt, W)
        upper = (row & half) != 0
        q_l = jnp.where(upper, c_q * jnp.exp(jnp.minimum(delta, 0.0)), 0.0).astype(BF16)
        k_l = jnp.where(upper, 0.0, kk * jnp.exp(jnp.minimum(-delta, 0.0))).astype(BF16)
        ct = lax.broadcasted_iota(jnp.int32, (HGRN_CHUNK, W), 0)
        cs = lax.broadcasted_iota(jnp.int32, (HGRN_CHUNK, W), 1) & (HGRN_CHUNK - 1)
        levels.append((q_l, k_l, (ct ^ cs) < blk))
        half = blk

    st = st_ref[...]
    o_parts = []
    for c in range(n_chunks):
        sl = slice(c * HGRN_CHUNK, (c + 1) * HGRN_CHUNK)
        o_c = _dot_nt(qe_b[sl], st.astype(BF16))
        scores = jnp.zeros((HGRN_CHUNK, W), F32)
        for q_l, k_l, pair_ok in levels:
            k_stack = jnp.where(same_head, jnp.concatenate([k_l[sl]] * HGRN_HEADS, axis=0), 0.0)
            scores = scores + jnp.where(pair_ok, _dot_nt(q_l[sl], k_stack), 0.0)
        v_stack = jnp.where(same_head, jnp.concatenate([v_b[sl]] * HGRN_HEADS, axis=0), 0.0)
        o_c = o_c + _dot(scores.astype(BF16), v_stack)
        o_parts.append(o_c)
        upd = _dot_tn(v_b[sl], kd_b[sl])
        st = st * decay_chunk[c] + jnp.where(same_head, upd, 0.0)
    st_ref[...] = st
    o = jnp.concatenate(o_parts, axis=0)

    rowd = row & (HGRN_DIAG - 1)
    for d in range(HGRN_DIAG):
        if d == 0:
            w_d, v_d = c_q * kk, v
        else:
            decay = jnp.exp(jnp.minimum(b - pltpu.roll(b, d, 0), 0.0))
            w_d = jnp.where(rowd >= d, c_q * pltpu.roll(kk, d, 0) * decay, 0.0)
            v_d = pltpu.roll(v, d, 0)
        o = o + _dot(w_d.astype(BF16), head_ones) * v_d

    ms = _dot((o * o).astype(BF16), head_ones) * (1.0 / HGRN_DK)
    y_c = o * lax.rsqrt(ms + EPS) * hnorm_ref[...] * jax.nn.silu(proj(6))

    s_b = proj(7)
    z = proj(8) * proj(9)
    ext = jnp.concatenate([tail_ref[2], z], axis=0)
    tail_ref[2] = z[tt - TAIL:]
    y_d = s_b * sum(sconvw_ref[k:k + 1, :] * _shift_rows(ext, SCONV_WIDTH - 1 - k, tt)
                    for k in range(SCONV_WIDTH))

    merged = jnp.zeros((tt, D_MODEL), F32)
    for k, y_k in enumerate((y_a, y_b, y_c, y_d)):
        lo = MIX_WIDTH + k * D_MODEL
        gate = jax.nn.sigmoid(_dot(u, win_ref[:, lo:lo + D_MODEL]))
        merged = merged + gate * _dot(y_k.astype(BF16), wbr_ref[k])
    mix = _dot(merged.astype(BF16), wout_ref[...])
    out_ref[0] = h_in + _rms(mix, gpost_ref[...])


def _mlp_kernel(h_ref, gpre_ref, gpost_ref, wup_ref, wdown_ref, out_ref):
    h_in = h_ref[...]
    u = _rms(h_in, gpre_ref[...]).astype(BF16)
    m = jnp.zeros(h_in.shape, F32)
    for j in range(D_FF // FF_CHUNK):
        cols = slice(j * FF_CHUNK, (j + 1) * FF_CHUNK)
        hid = jnp.square(jnp.maximum(_dot(u, wup_ref[:, cols]), 0.0))
        m = m + _dot(hid.astype(BF16), wdown_ref[cols, :])
    out_ref[...] = h_in + _rms(m, gpost_ref[...])


def _whole(arr):
    nd = arr.ndim
    return pl.BlockSpec(arr.shape, lambda *_: (0,) * nd)


def _block_diag(blocks):
    n, bi, bo = blocks.shape
    eye = jnp.eye(n, dtype=blocks.dtype)
    return jnp.einsum("nio,nm->nimo", blocks, eye).reshape(n * bi, n * bo)


def _mixer_layer(h, layer, params):
    bsz, seq, _ = h.shape
    row2 = lambda p: p.reshape(1, -1).astype(F32)
    operands = [
        h,
        row2(params["norm_mix_pre"]), row2(params["norm_mix_post"]),
        params["w_in"].astype(BF16),
        params["lru_conv_w"].astype(F32), row2(params["lru_conv_b"]),
        _block_diag(params["lru_w_a"]).astype(BF16), row2(params["lru_b_a"]),
        _block_diag(params["lru_w_x"]).astype(BF16), row2(params["lru_b_x"]),
        row2(params["lru_lambda"]),
        _block_diag(params["pool_w"]).astype(BF16), row2(params["pool_scale"]),
        params["hgrn_lower_bound_all"].astype(F32),
        row2(jnp.tile(params["hgrn_norm"], HGRN_HEADS)),
        params["sconv_w"].astype(F32),
        params["w_branch"].astype(BF16),
        params["w_out"].astype(BF16),
    ]
    in_specs = [pl.BlockSpec((1, TIME_TILE, D_MODEL), lambda bi, ti: (bi, ti, 0))]
    in_specs += [_whole(op) for op in operands[1:]]
    return pl.pallas_call(
        functools.partial(_mixer_kernel, layer),
        grid=(bsz, seq // TIME_TILE),
        in_specs=in_specs,
        out_specs=pl.BlockSpec((1, TIME_TILE, D_MODEL), lambda bi, ti: (bi, ti, 0)),
        out_shape=jax.ShapeDtypeStruct(h.shape, h.dtype),
        scratch_shapes=[
            pltpu.VMEM((3, TAIL, W), F32),
            pltpu.VMEM((8, W), F32),
            pltpu.VMEM((W, W), F32),
        ],
        compiler_params=pltpu.CompilerParams(
            dimension_semantics=("parallel", "arbitrary"),
            vmem_limit_bytes=VMEM_LIMIT_BYTES),
        name=f"mixer_l{layer}",
    )(*operands)


def _mlp_layer(h, layer, params):
    bsz, seq, d = h.shape
    h2 = h.reshape(bsz * seq, d)
    row2 = lambda p: p.reshape(1, -1).astype(F32)
    operands = [h2, row2(params["norm_mlp_pre"]), row2(params["norm_mlp_post"]),
                params["w_up"].astype(BF16), params["w_down"].astype(BF16)]
    in_specs = [pl.BlockSpec((MLP_TILE, d), lambda i: (i, 0))] + [_whole(op) for op in operands[1:]]
    out = pl.pallas_call(
        _mlp_kernel,
        grid=(bsz * seq // MLP_TILE,),
        in_specs=in_specs,
        out_specs=pl.BlockSpec((MLP_TILE, d), lambda i: (i, 0)),
        out_shape=jax.ShapeDtypeStruct(h2.shape, h2.dtype),
        compiler_params=pltpu.CompilerParams(
            dimension_semantics=("parallel",),
            vmem_limit_bytes=VMEM_LIMIT_BYTES),
        name=f"mlp_l{layer}",
    )(*operands)
    return out.reshape(bsz, seq, d)


def kernel(x, norm_mix_pre, norm_mix_post, norm_mlp_pre, norm_mlp_post, w_in, lru_conv_w, lru_conv_b,
           lru_w_a, lru_b_a, lru_w_x, lru_b_x, lru_lambda, pool_w, pool_scale, hgrn_lower_bound,
           hgrn_norm, sconv_w, w_branch, w_out, w_up, w_down):
    per_layer = dict(
        norm_mix_pre=norm_mix_pre, norm_mix_post=norm_mix_post, norm_mlp_pre=norm_mlp_pre,
        norm_mlp_post=norm_mlp_post, w_in=w_in, lru_conv_w=lru_conv_w, lru_conv_b=lru_conv_b,
        lru_w_a=lru_w_a, lru_b_a=lru_b_a, lru_w_x=lru_w_x, lru_b_x=lru_b_x, lru_lambda=lru_lambda,
        pool_w=pool_w, pool_scale=pool_scale, hgrn_norm=hgrn_norm, sconv_w=sconv_w,
        w_branch=w_branch, w_out=w_out, w_up=w_up, w_down=w_down)
    depth = w_in.shape[0]
    h = x
    for layer in range(depth):
        params = {name: p[layer] for name, p in per_layer.items()}
        params["hgrn_lower_bound_all"] = hgrn_lower_bound
        h = _mixer_layer(h, layer, params)
        h = _mlp_layer(h, layer, params)
    return h
```

```python
import functools

import jax
import jax.numpy as jnp
from jax import lax
from jax.experimental import pallas as pl
from jax.experimental.pallas import tpu as pltpu

D_MODEL = 1024
N_BRANCHES = 4
W = D_MODEL // N_BRANCHES
LRU_CONV = 4
LRU_C = 8.0
POOL_WINDOWS = (2, 4, 8, 16)
POOL_GROUP = W // len(POOL_WINDOWS)
HGRN_HEADS = 4
HGRN_DK = W // HGRN_HEADS
SCONV_WIDTH = 3
D_FF = 4 * D_MODEL
N_MIX_SLOTS = 10
MIX_WIDTH = N_MIX_SLOTS * W
EPS = 1e-6

TIME_TILE = 512
HGRN_CHUNK = 64
SUBLANES = 8
TAIL = 16
MLP_TILE = 512
FF_CHUNK = 1024
VMEM_LIMIT_BYTES = 56 * 1024 * 1024

F32 = jnp.float32
BF16 = jnp.bfloat16


def _dot(a, b):
    return jnp.dot(a, b, preferred_element_type=F32)


def _dot_nt(a, b):
    return lax.dot_general(a, b, (((1,), (1,)), ((), ())), preferred_element_type=F32)


def _dot_tn(a, b):
    return lax.dot_general(a, b, (((0,), (0,)), ((), ())), preferred_element_type=F32)


def _rms(x, gain):
    return x * lax.rsqrt(jnp.mean(x * x, axis=-1, keepdims=True) + EPS) * gain


def _group_roll(x, s):
    n, w = x.shape
    return pltpu.roll(x.reshape(n // SUBLANES, SUBLANES, w), s, 1).reshape(n, w)


def _shift_rows(ext, s, n):
    if s == 0:
        return ext[TAIL:TAIL + n]
    return pltpu.roll(ext, s, 0)[TAIL:TAIL + n]


def _mixer_kernel(lb_layer, h_ref, gpre_ref, gpost_ref, win_ref, convw_ref, convb_ref, wa_ref, ba_ref,
                  wx_ref, bx_ref, lam_ref, poolw_ref, pools_ref, lbraw_ref, hnorm_ref, sconvw_ref,
                  wbr_ref, wout_ref, out_ref, tail_ref, hlru_ref, st_ref):
    tt = h_ref.shape[1]
    n_chunks = tt // HGRN_CHUNK

    @pl.when(pl.program_id(1) == 0)
    def _():
        tail_ref[...] = jnp.zeros_like(tail_ref)
        hlru_ref[...] = jnp.zeros_like(hlru_ref)
        st_ref[...] = jnp.zeros_like(st_ref)

    h_in = h_ref[0]
    u = _rms(h_in, gpre_ref[...]).astype(BF16)

    def proj(slot):
        return _dot(u, win_ref[:, slot * W:(slot + 1) * W])

    row = lax.broadcasted_iota(jnp.int32, (tt, W), 0)
    lane = lax.broadcasted_iota(jnp.int32, (tt, W), 1)
    sub = row & (SUBLANES - 1)
    n_groups = tt // SUBLANES

    gate_cols = D_MODEL // W
    pending = [(k, j) for k in range(N_BRANCHES) for j in range(gate_cols)]
    gate_blocks = {}

    def fill(n=1):
        for _ in range(min(n, len(pending))):
            k, j = pending.pop(0)
            lo = MIX_WIDTH + k * D_MODEL + j * W
            gate_blocks[k, j] = jax.nn.sigmoid(_dot(u, win_ref[:, lo:lo + W]))

    def merge(acc, k, y_k):
        fill(sum(1 for kj in pending if kj[0] <= k))
        gate = jnp.concatenate([gate_blocks.pop((k, j)) for j in range(gate_cols)], axis=1)
        return acc + gate * _dot(y_k.astype(BF16), wbr_ref[k])

    a_x = proj(0)
    ext = jnp.concatenate([tail_ref[0], a_x], axis=0)
    tail_ref[0] = a_x[tt - TAIL:]
    xc = convb_ref[...] + sum(convw_ref[k:k + 1, :] * _shift_rows(ext, LRU_CONV - 1 - k, tt)
                              for k in range(LRU_CONV))
    fill()
    xc_b = xc.astype(BF16)
    r = jax.nn.sigmoid(_dot(xc_b, wa_ref[...]) + ba_ref[...])
    i_gate = jax.nn.sigmoid(_dot(xc_b, wx_ref[...]) + bx_ref[...])
    log_a = (-LRU_C) * r * jax.nn.softplus(-lam_ref[...])
    a = jnp.exp(log_a)
    mult = jnp.sqrt(jnp.tanh(-log_a) * (1.0 + a * a))
    uu = mult * i_gate * xc
    fill()
    s = 1
    while s < SUBLANES:
        keep = sub >= s
        a_sh = jnp.where(keep, _group_roll(a, s), 1.0)
        u_sh = jnp.where(keep, _group_roll(uu, s), 0.0)
        uu = uu + a * u_sh
        a = a * a_sh
        s *= 2
    fill()
    a_g = a.reshape(n_groups, SUBLANES, W)
    u_g = uu.reshape(n_groups, SUBLANES, W)
    h_prev = hlru_ref[0:1, :]
    h_groups = []
    for g in range(n_groups):
        h_g = u_g[g] + a_g[g] * h_prev
        h_groups.append(h_g)
        h_prev = h_g[SUBLANES - 1:SUBLANES, :]
    hlru_ref[0:1, :] = h_prev
    h_seq = jnp.concatenate(h_groups, axis=0)
    y_a = h_seq * jax.nn.gelu(proj(1))
    merged = merge(jnp.zeros((tt, D_MODEL), F32), 0, y_a)

    p_x = proj(2)
    ext = jnp.concatenate([tail_ref[1], p_x], axis=0)
    tail_ref[1] = p_x[tt - TAIL:]
    group = lax.shift_right_logical(lane, POOL_GROUP.bit_length() - 1)
    win_sum = ext
    mean = jnp.zeros((tt, W), F32)
    pos = (pl.program_id(1) * tt + row + 1).astype(F32)
    for g, win in enumerate(POOL_WINDOWS):
        win_sum = win_sum + pltpu.roll(win_sum, win // 2, 0)
        mean = jnp.where(group == g, win_sum[TAIL:] / jnp.minimum(pos, float(win)), mean)
    fill()
    y_b = _dot((mean - p_x).astype(BF16), poolw_ref[...]) * pools_ref[...]
    merged = merge(merged, 1, y_b)

    c_q = proj(3)
    c_f = proj(4)
    v = proj(5)
    lb_sm = jax.nn.softmax(lbraw_ref[...], axis=0)
    lb = jnp.sum(lb_sm[1:lb_layer + 1], axis=0, keepdims=True) if lb_layer else jnp.zeros((1, W), F32)
    f = lb + (1.0 - lb) * jax.nn.sigmoid(c_f)
    lf = jnp.log(f)
    kk = 1.0 - f
    b = lf
    s = 1
    while s < SUBLANES:
        b = b + jnp.where(sub >= s, _group_roll(b, s), 0.0)
        s *= 2
    b_g = b.reshape(n_groups, SUBLANES, W)
    b_parts = []
    for g in range(n_groups):
        if g % (HGRN_CHUNK // SUBLANES) == 0:
            b_parts.append(b_g[g])
        else:
            b_parts.append(b_g[g] + b_parts[-1][SUBLANES - 1:SUBLANES, :])
    b = jnp.concatenate(b_parts, axis=0)
    fill()
    b3 = b.reshape(n_chunks, HGRN_CHUNK, W)
    b_last = b3[:, HGRN_CHUNK - 1:HGRN_CHUNK, :]
    qe_b = (c_q * jnp.exp(b)).astype(BF16)
    kd_b = (kk.reshape(n_chunks, HGRN_CHUNK, W) * jnp.exp(b_last - b3)).reshape(tt, W).astype(BF16)
    decay_chunk = jnp.exp(b_last)
    v_b = v.astype(BF16)

    sq_r = lax.broadcasted_iota(jnp.int32, (W, W), 0)
    sq_c = lax.broadcasted_iota(jnp.int32, (W, W), 1)
    same_head = (sq_r ^ sq_c) < HGRN_DK
    head_ones = jnp.where(same_head, 1.0, 0.0).astype(BF16)

    pair_xor = (lax.broadcasted_iota(jnp.int32, (HGRN_CHUNK, W), 0)
                ^ (lax.broadcasted_iota(jnp.int32, (HGRN_CHUNK, W), 1) & (HGRN_CHUNK - 1)))
    levels = []
    half = SUBLANES
    while half < HGRN_CHUNK:
        blk = 2 * half
        upper = (row & half) != 0
        bb = b.reshape(tt // blk, blk, W)
        delta = (bb - bb[:, half - 1:half, :]).reshape(tt, W)
        q_l = jnp.where(upper, c_q * jnp.exp(jnp.minimum(delta, 0.0)), 0.0)
        k_l = jnp.where(upper, 0.0, kk * jnp.exp(jnp.minimum(-delta, 0.0)))
        levels.append((q_l.astype(BF16), k_l.astype(BF16), blk))
        half = blk

    st = st_ref[...]
    o_parts = []
    for c in range(n_chunks):
        sl = slice(c * HGRN_CHUNK, (c + 1) * HGRN_CHUNK)
        o_c = _dot_nt(qe_b[sl], st.astype(BF16))
        scores = None
        for q_l, k_l, xor_bound in reversed(levels):
            k_stack = jnp.where(same_head, jnp.concatenate([k_l[sl]] * HGRN_HEADS, axis=0), 0.0)
            level_scores = _dot_nt(q_l[sl], k_stack)
            scores = level_scores if scores is None else jnp.where(pair_xor < xor_bound, level_scores, scores)
        scores = jnp.where(pair_xor < SUBLANES, 0.0, scores)
        fill()
        v_stack = jnp.where(same_head, jnp.concatenate([v_b[sl]] * HGRN_HEADS, axis=0), 0.0)
        o_c = o_c + _dot(scores.astype(BF16), v_stack)
        o_parts.append(o_c)
        upd = _dot_tn(v_b[sl], kd_b[sl])
        st = st * decay_chunk[c] + jnp.where(same_head, upd, 0.0)
        if c % 2 == 1:
            fill()
    st_ref[...] = st
    o = jnp.concatenate(o_parts, axis=0)

    decay = None
    f_d = f
    for d in range(SUBLANES):
        if d == 0:
            w_d, v_d = c_q * kk, v
        else:
            decay = f if d == 1 else decay * f_d
            f_d = _group_roll(f, d)
            w_d = jnp.where(sub >= d, c_q * (1.0 - f_d) * decay, 0.0)
            v_d = _group_roll(v, d)
        o = o + _dot(w_d.astype(BF16), head_ones) * v_d
        fill()

    ms = _dot((o * o).astype(BF16), head_ones) * (1.0 / HGRN_DK)
    y_c = o * lax.rsqrt(ms + EPS) * hnorm_ref[...] * jax.nn.silu(proj(6))
    merged = merge(merged, 2, y_c)

    s_b = proj(7)
    z = proj(8) * proj(9)
    ext = jnp.concatenate([tail_ref[2], z], axis=0)
    tail_ref[2] = z[tt - TAIL:]
    y_d = s_b * sum(sconvw_ref[k:k + 1, :] * _shift_rows(ext, SCONV_WIDTH - 1 - k, tt)
                    for k in range(SCONV_WIDTH))

    merged = merge(merged, 3, y_d)
    mix =_dot(merged.astype(BF16), wout_ref[...])
    out_ref[0] = h_in + _rms(mix, gpost_ref[...])


def _mlp_kernel(h_ref, gpre_ref, gpost_ref, wup_ref, wdown_ref, out_ref):
    h_in = h_ref[...]
    u = _rms(h_in, gpre_ref[...]).astype(BF16)
    m = jnp.zeros(h_in.shape, F32)
    for j in range(D_FF // FF_CHUNK):
        cols = slice(j * FF_CHUNK, (j + 1) * FF_CHUNK)
        hid = jnp.square(jnp.maximum(_dot(u, wup_ref[:, cols]), 0.0))
        m = m + _dot(hid.astype(BF16), wdown_ref[cols, :])
    out_ref[...] = h_in + _rms(m, gpost_ref[...])


def _whole(arr):
    nd = arr.ndim
    return pl.BlockSpec(arr.shape, lambda *_: (0,) * nd)


def _block_diag(blocks):
    n, bi, bo = blocks.shape
    eye = jnp.eye(n, dtype=blocks.dtype)
    return jnp.einsum("nio,nm->nimo", blocks, eye).reshape(n * bi, n * bo)


def _mixer_layer(h, layer, params):
    bsz, seq, _ = h.shape
    row2 = lambda p: p.reshape(1, -1).astype(F32)
    operands = [
        h,
        row2(params["norm_mix_pre"]), row2(params["norm_mix_post"]),
        params["w_in"].astype(BF16),
        params["lru_conv_w"].astype(F32), row2(params["lru_conv_b"]),
        _block_diag(params["lru_w_a"]).astype(BF16), row2(params["lru_b_a"]),
        _block_diag(params["lru_w_x"]).astype(BF16), row2(params["lru_b_x"]),
        row2(params["lru_lambda"]),
        _block_diag(params["pool_w"]).astype(BF16), row2(params["pool_scale"]),
        params["hgrn_lower_bound_all"].astype(F32),
        row2(jnp.tile(params["hgrn_norm"], HGRN_HEADS)),
        params["sconv_w"].astype(F32),
        params["w_branch"].astype(BF16),
        params["w_out"].astype(BF16),
    ]
    in_specs = [pl.BlockSpec((1, TIME_TILE, D_MODEL), lambda bi, ti: (bi, ti, 0))]
    in_specs += [_whole(op) for op in operands[1:]]
    return pl.pallas_call(
        functools.partial(_mixer_kernel, layer),
        grid=(bsz, seq // TIME_TILE),
        in_specs=in_specs,
        out_specs=pl.BlockSpec((1, TIME_TILE, D_MODEL), lambda bi, ti: (bi, ti, 0)),
        out_shape=jax.ShapeDtypeStruct(h.shape, h.dtype),
        scratch_shapes=[
            pltpu.VMEM((3, TAIL, W), F32),
            pltpu.VMEM((8, W), F32),
            pltpu.VMEM((W, W), F32),
        ],
        compiler_params=pltpu.CompilerParams(
            dimension_semantics=("parallel", "arbitrary"),
            vmem_limit_bytes=VMEM_LIMIT_BYTES),
        name=f"mixer_l{layer}",
    )(*operands)


def _mlp_layer(h, layer, params):
    bsz, seq, d = h.shape
    h2 = h.reshape(bsz * seq, d)
    row2 = lambda p: p.reshape(1, -1).astype(F32)
    operands = [h2, row2(params["norm_mlp_pre"]), row2(params["norm_mlp_post"]),
                params["w_up"].astype(BF16), params["w_down"].astype(BF16)]
    in_specs = [pl.BlockSpec((MLP_TILE, d), lambda i: (i, 0))] + [_whole(op) for op in operands[1:]]
    out = pl.pallas_call(
        _mlp_kernel,
        grid=(bsz * seq // MLP_TILE,),
        in_specs=in_specs,
        out_specs=pl.BlockSpec((MLP_TILE, d), lambda i: (i, 0)),
        out_shape=jax.ShapeDtypeStruct(h2.shape, h2.dtype),
        compiler_params=pltpu.CompilerParams(
            dimension_semantics=("parallel",),
            vmem_limit_bytes=VMEM_LIMIT_BYTES),
        name=f"mlp_l{layer}",
    )(*operands)
    return out.reshape(bsz, seq, d)


def kernel(x, norm_mix_pre, norm_mix_post, norm_mlp_pre, norm_mlp_post, w_in, lru_conv_w, lru_conv_b,
           lru_w_a, lru_b_a, lru_w_x, lru_b_x, lru_lambda, pool_w, pool_scale, hgrn_lower_bound,
           hgrn_norm, sconv_w, w_branch, w_out, w_up, w_down):
    per_layer = dict(
        norm_mix_pre=norm_mix_pre, norm_mix_post=norm_mix_post, norm_mlp_pre=norm_mlp_pre,
        norm_mlp_post=norm_mlp_post, w_in=w_in, lru_conv_w=lru_conv_w, lru_conv_b=lru_conv_b,
        lru_w_a=lru_w_a, lru_b_a=lru_b_a, lru_w_x=lru_w_x, lru_b_x=lru_b_x, lru_lambda=lru_lambda,
        pool_w=pool_w, pool_scale=pool_scale, hgrn_norm=hgrn_norm, sconv_w=sconv_w,
        w_branch=w_branch, w_out=w_out, w_up=w_up, w_down=w_down)
    depth = w_in.shape[0]
    h = x
    for layer in range(depth):
        params = {name: p[layer] for name, p in per_layer.items()}
        params["hgrn_lower_bound_all"] = hgrn_lower_bound
        h = _mixer_layer(h, layer, params)
        h = _mlp_layer(h, layer, params)
    return h
```

```python
import functools

import jax
import jax.numpy as jnp
from jax import lax
from jax.experimental import pallas as pl
from jax.experimental.pallas import tpu as pltpu

D_MODEL = 1024
N_BRANCHES = 4
W = D_MODEL // N_BRANCHES
LRU_CONV = 4
LRU_C = 8.0
POOL_WINDOWS = (2, 4, 8, 16)
POOL_GROUP = W // len(POOL_WINDOWS)
HGRN_HEADS = 4
HGRN_DK = W // HGRN_HEADS
SCONV_WIDTH = 3
D_FF = 4 * D_MODEL
N_MIX_SLOTS = 10
MIX_WIDTH = N_MIX_SLOTS * W
EPS = 1e-6

TIME_TILE = 512
HGRN_CHUNK = 64
SUBLANES = 8
EDGE_BLOCKS = 4
TAIL = 16
MLP_TILE = 512
MLP_ROW_BLOCKS = 2
FF_CHUNK = 1024
VMEM_LIMIT_BYTES = 56 * 1024 * 1024

F32 = jnp.float32
BF16 = jnp.bfloat16


def _dot(a, b):
    return jnp.dot(a, b, preferred_element_type=F32)


def _dot_nt(a, b):
    return lax.dot_general(a, b, (((1,), (1,)), ((), ())), preferred_element_type=F32)


def _dot_tn(a, b):
    return lax.dot_general(a, b, (((0,), (0,)), ((), ())), preferred_element_type=F32)


def _rms(x, gain):
    return x * lax.rsqrt(jnp.mean(x * x, axis=-1, keepdims=True) + EPS) * gain


def _group_roll(x, s):
    n, w = x.shape
    return pltpu.roll(x.reshape(n // SUBLANES, SUBLANES, w), s, 1).reshape(n, w)


def _shift_rows(ext, s, n):
    if s == 0:
        return ext[TAIL:TAIL + n]
    return pltpu.roll(ext, s, 0)[TAIL:TAIL + n]


def _mixer_kernel(lb_layer, h_ref, gpre_ref, gpost_ref, win_ref, convw_ref, convb_ref, wa_ref, ba_ref,
                  wx_ref, bx_ref, lam_ref, poolw_ref, pools_ref, lbraw_ref, hnorm_ref, sconvw_ref,
                  wbr_ref, wout_ref, out_ref, tail_ref, hlru_ref, st_ref):
    tt = h_ref.shape[1]
    n_chunks = tt // HGRN_CHUNK

    @pl.when(pl.program_id(1) == 0)
    def _():
        tail_ref[...] = jnp.zeros_like(tail_ref)
        hlru_ref[...] = jnp.zeros_like(hlru_ref)
        st_ref[...] = jnp.zeros_like(st_ref)

    row_blocks = [slice(i * tt // EDGE_BLOCKS, (i + 1) * tt // EDGE_BLOCKS) for i in range(EDGE_BLOCKS)]
    u_blocks = [_rms(h_ref[0, rows, :], gpre_ref[...]).astype(BF16) for rows in row_blocks]
    u = jnp.concatenate(u_blocks, axis=0)

    def proj(slot):
        w_slot = win_ref[:, slot * W:(slot + 1) * W]
        if slot == 0:
            return jnp.concatenate([_dot(u_blk, w_slot) for u_blk in u_blocks], axis=0)
        return _dot(u, w_slot)

    row = lax.broadcasted_iota(jnp.int32, (tt, W), 0)
    lane = lax.broadcasted_iota(jnp.int32, (tt, W), 1)
    sub = row & (SUBLANES - 1)
    n_groups = tt // SUBLANES

    gate_cols = D_MODEL // W
    pending = [(k, j) for k in range(N_BRANCHES) for j in range(gate_cols)]
    gate_blocks = {}

    def fill(n=1):
        for _ in range(min(n, len(pending))):
            k, j = pending.pop(0)
            lo = MIX_WIDTH + k * D_MODEL + j * W
            gate_blocks[k, j] = jax.nn.sigmoid(_dot(u, win_ref[:, lo:lo + W]))

    def merge(acc, k, y_k):
        fill(sum(1 for kj in pending if kj[0] <= k))
        gate = jnp.concatenate([gate_blocks.pop((k, j)) for j in range(gate_cols)], axis=1)
        return acc + gate * _dot(y_k.astype(BF16), wbr_ref[k])

    a_x = proj(0)
    ext = jnp.concatenate([tail_ref[0], a_x], axis=0)
    tail_ref[0] = a_x[tt - TAIL:]
    xc = convb_ref[...] + sum(convw_ref[k:k + 1, :] * _shift_rows(ext, LRU_CONV - 1 - k, tt)
                              for k in range(LRU_CONV))
    fill()
    xc_b = xc.astype(BF16)
    r = jax.nn.sigmoid(_dot(xc_b, wa_ref[...]) + ba_ref[...])
    i_gate = jax.nn.sigmoid(_dot(xc_b, wx_ref[...]) + bx_ref[...])
    log_a = (-LRU_C) * r * jax.nn.softplus(-lam_ref[...])
    a = jnp.exp(log_a)
    mult = jnp.sqrt(jnp.tanh(-log_a) * (1.0 + a * a))
    uu = mult * i_gate * xc
    fill()
    s = 1
    while s < SUBLANES:
        keep = sub >= s
        a_sh = jnp.where(keep, _group_roll(a, s), 1.0)
        u_sh = jnp.where(keep, _group_roll(uu, s), 0.0)
        uu = uu + a * u_sh
        a = a * a_sh
        s *= 2
    fill()
    a_g = a.reshape(n_groups, SUBLANES, W)
    u_g = uu.reshape(n_groups, SUBLANES, W)
    h_prev = hlru_ref[0:1, :]
    h_groups = []
    for g in range(n_groups):
        h_g = u_g[g] + a_g[g] * h_prev
        h_groups.append(h_g)
        h_prev = h_g[SUBLANES - 1:SUBLANES, :]
    hlru_ref[0:1, :] = h_prev
    h_seq = jnp.concatenate(h_groups, axis=0)
    y_a = h_seq * jax.nn.gelu(proj(1))
    merged = merge(jnp.zeros((tt, D_MODEL), F32), 0, y_a)

    p_x = proj(2)
    ext = jnp.concatenate([tail_ref[1], p_x], axis=0)
    tail_ref[1] = p_x[tt - TAIL:]
    group = lax.shift_right_logical(lane, POOL_GROUP.bit_length() - 1)
    win_sum = ext
    mean = jnp.zeros((tt, W), F32)
    pos = (pl.program_id(1) * tt + row + 1).astype(F32)
    for g, win in enumerate(POOL_WINDOWS):
        win_sum = win_sum + pltpu.roll(win_sum, win // 2, 0)
        mean = jnp.where(group == g, win_sum[TAIL:] / jnp.minimum(pos, float(win)), mean)
    fill()
    y_b = _dot((mean - p_x).astype(BF16), poolw_ref[...]) * pools_ref[...]
    merged = merge(merged, 1, y_b)

    c_q = proj(3)
    c_f = proj(4)
    v = proj(5)
    lb_sm = jax.nn.softmax(lbraw_ref[...], axis=0)
    lb = jnp.sum(lb_sm[1:lb_layer + 1], axis=0, keepdims=True) if lb_layer else jnp.zeros((1, W), F32)
    f = lb + (1.0 - lb) * jax.nn.sigmoid(c_f)
    lf = jnp.log(f)
    kk = 1.0 - f
    b = lf
    s = 1
    while s < SUBLANES:
        b = b + jnp.where(sub >= s, _group_roll(b, s), 0.0)
        s *= 2
    b_g = b.reshape(n_groups, SUBLANES, W)
    b_parts = []
    for g in range(n_groups):
        if g % (HGRN_CHUNK // SUBLANES) == 0:
            b_parts.append(b_g[g])
        else:
            b_parts.append(b_g[g] + b_parts[-1][SUBLANES - 1:SUBLANES, :])
    b = jnp.concatenate(b_parts, axis=0)
    fill()
    b3 = b.reshape(n_chunks, HGRN_CHUNK, W)
    b_last = b3[:, HGRN_CHUNK - 1:HGRN_CHUNK, :]
    qe_b = (c_q * jnp.exp(b)).astype(BF16)
    kd_b = (kk.reshape(n_chunks, HGRN_CHUNK, W) * jnp.exp(b_last - b3)).reshape(tt, W).astype(BF16)
    decay_chunk = jnp.exp(b_last)
    v_b = v.astype(BF16)

    sq_r = lax.broadcasted_iota(jnp.int32, (W, W), 0)
    sq_c = lax.broadcasted_iota(jnp.int32, (W, W), 1)
    same_head = (sq_r ^ sq_c) < HGRN_DK
    head_ones = jnp.where(same_head, 1.0, 0.0).astype(BF16)

    pair_xor = (lax.broadcasted_iota(jnp.int32, (HGRN_CHUNK, W), 0)
                ^ (lax.broadcasted_iota(jnp.int32, (HGRN_CHUNK, W), 1) & (HGRN_CHUNK - 1)))
    levels = []
    half = SUBLANES
    while half < HGRN_CHUNK:
        blk = 2 * half
        upper = (row & half) != 0
        bb = b.reshape(tt // blk, blk, W)
        delta = (bb - bb[:, half - 1:half, :]).reshape(tt, W)
        q_l = jnp.where(upper, c_q * jnp.exp(jnp.minimum(delta, 0.0)), 0.0)
        k_l = jnp.where(upper, 0.0, kk * jnp.exp(jnp.minimum(-delta, 0.0)))
        levels.append((q_l.astype(BF16), k_l.astype(BF16), blk))
        half = blk

    chunk_rows = [slice(c * HGRN_CHUNK, (c + 1) * HGRN_CHUNK) for c in range(n_chunks)]

    def head_stack(x_chunk):
        return jnp.where(same_head, jnp.concatenate([x_chunk] * HGRN_HEADS, axis=0), 0.0)

    updates = []
    for c, sl in enumerate(chunk_rows):
        updates.append(_dot_tn(v_b[sl], kd_b[sl]))
        if c % 2 == 1:
            fill()
    probs = []
    for c, sl in enumerate(chunk_rows):
        scores = None
        for q_l, k_l, xor_bound in reversed(levels):
            level_scores = _dot_nt(q_l[sl], head_stack(k_l[sl]))
            scores = level_scores if scores is None else jnp.where(pair_xor < xor_bound, level_scores, scores)
        probs.append(jnp.where(pair_xor < SUBLANES, 0.0, scores).astype(BF16))
        fill()
    st = st_ref[...]
    o_parts = []
    for c, sl in enumerate(chunk_rows):
        o_c = _dot_nt(qe_b[sl], st.astype(BF16))
        o_parts.append(o_c + _dot(probs[c], head_stack(v_b[sl])))
        st = st * decay_chunk[c] + jnp.where(same_head, updates[c], 0.0)
    st_ref[...] = st
    o = jnp.concatenate(o_parts, axis=0)

    decay = None
    f_d = f
    for d in range(SUBLANES):
        if d == 0:
            w_d, v_d = c_q * kk, v
        else:
            decay = f if d == 1 else decay * f_d
            f_d = _group_roll(f, d)
            w_d = jnp.where(sub >= d, c_q * (1.0 - f_d) * decay, 0.0)
            v_d = _group_roll(v, d)
        o = o + _dot(w_d.astype(BF16), head_ones) * v_d
        fill()

    ms = _dot((o * o).astype(BF16), head_ones) * (1.0 / HGRN_DK)
    y_c = o * lax.rsqrt(ms + EPS) * hnorm_ref[...] * jax.nn.silu(proj(6))
    merged = merge(merged, 2, y_c)

    s_b = proj(7)
    z = proj(8) * proj(9)
    ext = jnp.concatenate([tail_ref[2], z], axis=0)
    tail_ref[2] = z[tt - TAIL:]
    y_d = s_b * sum(sconvw_ref[k:k + 1, :] * _shift_rows(ext, SCONV_WIDTH - 1 - k, tt)
                    for k in range(SCONV_WIDTH))

    merged = merge(merged, 3, y_d)
    merged_b = merged.astype(BF16)
    for rows in row_blocks:
        mix = _dot(merged_b[rows], wout_ref[...])
        out_ref[0, rows, :] = h_ref[0, rows, :] + _rms(mix, gpost_ref[...])


def _mlp_kernel(h_ref, gpre_ref, gpost_ref, wup_ref, wdown_ref, out_ref):
    n_rows = h_ref.shape[0]
    for i in range(MLP_ROW_BLOCKS):
        rows = slice(i * n_rows // MLP_ROW_BLOCKS, (i + 1) * n_rows // MLP_ROW_BLOCKS)
        h_in = h_ref[rows, :]
        u = _rms(h_in, gpre_ref[...]).astype(BF16)
        m = jnp.zeros(h_in.shape, F32)
        for j in range(D_FF // FF_CHUNK):
            cols = slice(j * FF_CHUNK, (j + 1) * FF_CHUNK)
            hid = jnp.square(jnp.maximum(_dot(u, wup_ref[:, cols]), 0.0))
            m = m + _dot(hid.astype(BF16), wdown_ref[cols, :])
        out_ref[rows, :] = h_in + _rms(m, gpost_ref[...])


def _whole(arr):
    nd = arr.ndim
    return pl.BlockSpec(arr.shape, lambda *_: (0,) * nd)


def _block_diag(blocks):
    n, bi, bo = blocks.shape
    eye = jnp.eye(n, dtype=blocks.dtype)
    return jnp.einsum("nio,nm->nimo", blocks, eye).reshape(n * bi, n * bo)


def _mixer_layer(h, layer, params):
    bsz, seq, _ = h.shape
    row2 = lambda p: p.reshape(1, -1).astype(F32)
    operands = [
        h,
        row2(params["norm_mix_pre"]), row2(params["norm_mix_post"]),
        params["w_in"].astype(BF16),
        params["lru_conv_w"].astype(F32), row2(params["lru_conv_b"]),
        _block_diag(params["lru_w_a"]).astype(BF16), row2(params["lru_b_a"]),
        _block_diag(params["lru_w_x"]).astype(BF16), row2(params["lru_b_x"]),
        row2(params["lru_lambda"]),
        _block_diag(params["pool_w"]).astype(BF16), row2(params["pool_scale"]),
        params["hgrn_lower_bound_all"].astype(F32),
        row2(jnp.tile(params["hgrn_norm"], HGRN_HEADS)),
        params["sconv_w"].astype(F32),
        params["w_branch"].astype(BF16),
        params["w_out"].astype(BF16),
    ]
    in_specs = [pl.BlockSpec((1, TIME_TILE, D_MODEL), lambda bi, ti: (bi, ti, 0))]
    in_specs += [_whole(op) for op in operands[1:]]
    return pl.pallas_call(
        functools.partial(_mixer_kernel, layer),
        grid=(bsz, seq // TIME_TILE),
        in_specs=in_specs,
        out_specs=pl.BlockSpec((1, TIME_TILE, D_MODEL), lambda bi, ti: (bi, ti, 0)),
        out_shape=jax.ShapeDtypeStruct(h.shape, h.dtype),
        scratch_shapes=[
            pltpu.VMEM((3, TAIL, W), F32),
            pltpu.VMEM((8, W), F32),
            pltpu.VMEM((W, W), F32),
        ],
        compiler_params=pltpu.CompilerParams(
            dimension_semantics=("parallel", "arbitrary"),
            vmem_limit_bytes=VMEM_LIMIT_BYTES),
        name=f"mixer_l{layer}",
    )(*operands)


def _mlp_layer(h, layer, params):
    bsz, seq, d = h.shape
    h2 = h.reshape(bsz * seq, d)
    row2 = lambda p: p.reshape(1, -1).astype(F32)
    operands = [h2, row2(params["norm_mlp_pre"]), row2(params["norm_mlp_post"]),
                params["w_up"].astype(BF16), params["w_down"].astype(BF16)]
    in_specs = [pl.BlockSpec((MLP_TILE, d), lambda i: (i, 0))] + [_whole(op) for op in operands[1:]]
    out = pl.pallas_call(
        _mlp_kernel,
        grid=(bsz * seq // MLP_TILE,),
        in_specs=in_specs,
        out_specs=pl.BlockSpec((MLP_TILE, d), lambda i: (i, 0)),
        out_shape=jax.ShapeDtypeStruct(h2.shape, h2.dtype),
        compiler_params=pltpu.CompilerParams(
            dimension_semantics=("parallel",),
            vmem_limit_bytes=VMEM_LIMIT_BYTES),
        name=f"mlp_l{layer}",
    )(*operands)
    return out.reshape(bsz, seq, d)


def kernel(x, norm_mix_pre, norm_mix_post, norm_mlp_pre, norm_mlp_post, w_in, lru_conv_w, lru_conv_b,
           lru_w_a, lru_b_a, lru_w_x, lru_b_x, lru_lambda, pool_w, pool_scale, hgrn_lower_bound,
           hgrn_norm, sconv_w, w_branch, w_out, w_up, w_down):
    per_layer = dict(
        norm_mix_pre=norm_mix_pre, norm_mix_post=norm_mix_post, norm_mlp_pre=norm_mlp_pre,
        norm_mlp_post=norm_mlp_post, w_in=w_in, lru_conv_w=lru_conv_w, lru_conv_b=lru_conv_b,
        lru_w_a=lru_w_a, lru_b_a=lru_b_a, lru_w_x=lru_w_x, lru_b_x=lru_b_x, lru_lambda=lru_lambda,
        pool_w=pool_w, pool_scale=pool_scale, hgrn_norm=hgrn_norm, sconv_w=sconv_w,
        w_branch=w_branch, w_out=w_out, w_up=w_up, w_down=w_down)
    depth = w_in.shape[0]
    h = x
    for layer in range(depth):
        params = {name: p[layer] for name, p in per_layer.items()}
        params["hgrn_lower_bound_all"] = hgrn_lower_bound
        h = _mixer_layer(h, layer, params)
        h = _mlp_layer(h, layer, params)
    return h
```

```python
import functools

import jax
import jax.numpy as jnp
from jax import lax
from jax.experimental import pallas as pl
from jax.experimental.pallas import tpu as pltpu

D_MODEL = 1024
N_BRANCHES = 4
W = D_MODEL // N_BRANCHES
LRU_CONV = 4
LRU_C = 8.0
POOL_WINDOWS = (2, 4, 8, 16)
POOL_GROUP = W // len(POOL_WINDOWS)
HGRN_HEADS = 4
HGRN_DK = W // HGRN_HEADS
SCONV_WIDTH = 3
D_FF = 4 * D_MODEL
N_MIX_SLOTS = 10
MIX_WIDTH = N_MIX_SLOTS * W
EPS = 1e-6

TIME_TILE = 512
HGRN_CHUNK = 64
SUBLANES = 8
EDGE_BLOCKS = 4
TAIL = 16
MLP_TILE = 512
MLP_ROW_BLOCKS = 2
FF_CHUNK = 1024
VMEM_LIMIT_BYTES = 56 * 1024 * 1024

F32 = jnp.float32
BF16 = jnp.bfloat16


def _dot(a, b):
    return jnp.dot(a, b, preferred_element_type=F32)


def _dot_nt(a, b):
    return lax.dot_general(a, b, (((1,), (1,)), ((), ())), preferred_element_type=F32)


def _dot_tn(a, b):
    return lax.dot_general(a, b, (((0,), (0,)), ((), ())), preferred_element_type=F32)


def _rms(x, gain):
    return x * lax.rsqrt(jnp.mean(x * x, axis=-1, keepdims=True) + EPS) * gain


def _group_roll(x, s):
    n, w = x.shape
    return pltpu.roll(x.reshape(n // SUBLANES, SUBLANES, w), s, 1).reshape(n, w)


def _shift_rows(ext, s, n):
    if s == 0:
        return ext[TAIL:TAIL + n]
    return pltpu.roll(ext, s, 0)[TAIL:TAIL + n]


def _mixer_kernel(lb_layer, h_ref, gpre_ref, gpost_ref, win_ref, convw_ref, convb_ref, wa_ref, ba_ref,
                  wx_ref, bx_ref, lam_ref, poolw_ref, pools_ref, lbraw_ref, hnorm_ref, sconvw_ref,
                  wbr_ref, wout_ref, out_ref, tail_ref, hlru_ref, st_ref):
    tt = h_ref.shape[1]
    n_chunks = tt // HGRN_CHUNK

    @pl.when(pl.program_id(1) == 0)
    def _():
        tail_ref[...] = jnp.zeros_like(tail_ref)
        hlru_ref[...] = jnp.zeros_like(hlru_ref)
        st_ref[...] = jnp.zeros_like(st_ref)

    row_blocks = [slice(i * tt // EDGE_BLOCKS, (i + 1) * tt // EDGE_BLOCKS) for i in range(EDGE_BLOCKS)]
    u_blocks = [_rms(h_ref[0, rows, :], gpre_ref[...]).astype(BF16) for rows in row_blocks]
    u = jnp.concatenate(u_blocks, axis=0)

    def proj(slot):
        w_slot = win_ref[:, slot * W:(slot + 1) * W]
        if slot == 0:
            return jnp.concatenate([_dot(u_blk, w_slot) for u_blk in u_blocks], axis=0)
        return _dot(u, w_slot)

    row = lax.broadcasted_iota(jnp.int32, (tt, W), 0)
    lane = lax.broadcasted_iota(jnp.int32, (tt, W), 1)
    sub = row & (SUBLANES - 1)
    n_groups = tt // SUBLANES

    gate_cols = D_MODEL // W
    pending = [(k, j) for k in range(N_BRANCHES) for j in range(gate_cols)]
    gate_blocks = {}

    def fill(n=1):
        for _ in range(min(n, len(pending))):
            k, j = pending.pop(0)
            lo = MIX_WIDTH + k * D_MODEL + j * W
            gate_blocks[k, j] = jax.nn.sigmoid(_dot(u, win_ref[:, lo:lo + W]))

    def merge(acc, k, y_k):
        fill(sum(1 for kj in pending if kj[0] <= k))
        gate = jnp.concatenate([gate_blocks.pop((k, j)) for j in range(gate_cols)], axis=1)
        return acc + gate * _dot(y_k.astype(BF16), wbr_ref[k])

    a_x = proj(0)
    ext = jnp.concatenate([tail_ref[0], a_x], axis=0)
    tail_ref[0] = a_x[tt - TAIL:]
    xc = convb_ref[...] + sum(convw_ref[k:k + 1, :] * _shift_rows(ext, LRU_CONV - 1 - k, tt)
                              for k in range(LRU_CONV))
    fill()
    xc_b = xc.astype(BF16)
    r = jax.nn.sigmoid(_dot(xc_b, wa_ref[...]) + ba_ref[...])
    i_gate = jax.nn.sigmoid(_dot(xc_b, wx_ref[...]) + bx_ref[...])
    log_a = (-LRU_C) * r * jax.nn.softplus(-lam_ref[...])
    a = jnp.exp(log_a)
    mult = jnp.sqrt(jnp.tanh(-log_a) * (1.0 + a * a))
    uu = mult * i_gate * xc
    fill()
    s = 1
    while s < SUBLANES:
        keep = sub >= s
        a_sh = jnp.where(keep, _group_roll(a, s), 1.0)
        u_sh = jnp.where(keep, _group_roll(uu, s), 0.0)
        uu = uu + a * u_sh
        a = a * a_sh
        s *= 2
    fill()
    a_g = a.reshape(n_groups, SUBLANES, W)
    u_g = uu.reshape(n_groups, SUBLANES, W)
    h_prev = hlru_ref[0:1, :]
    h_groups = []
    for g in range(n_groups):
        h_g = u_g[g] + a_g[g] * h_prev
        h_groups.append(h_g)
        h_prev = h_g[SUBLANES - 1:SUBLANES, :]
    hlru_ref[0:1, :] = h_prev
    h_seq = jnp.concatenate(h_groups, axis=0)
    y_a = h_seq * jax.nn.gelu(proj(1))
    merged = merge(jnp.zeros((tt, D_MODEL), F32), 0, y_a)

    p_x = proj(2)
    ext = jnp.concatenate([tail_ref[1], p_x], axis=0)
    tail_ref[1] = p_x[tt - TAIL:]
    group = lax.shift_right_logical(lane, POOL_GROUP.bit_length() - 1)
    win_sum = ext
    mean = jnp.zeros((tt, W), F32)
    pos = (pl.program_id(1) * tt + row + 1).astype(F32)
    for g, win in enumerate(POOL_WINDOWS):
        win_sum = win_sum + pltpu.roll(win_sum, win // 2, 0)
        mean = jnp.where(group == g, win_sum[TAIL:] / jnp.minimum(pos, float(win)), mean)
    fill()
    y_b = _dot((mean - p_x).astype(BF16), poolw_ref[...]) * pools_ref[...]
    merged = merge(merged, 1, y_b)

    c_q = proj(3)
    c_f = proj(4)
    v = proj(5)
    lb_sm = jax.nn.softmax(lbraw_ref[...], axis=0)
    lb = jnp.sum(lb_sm[1:lb_layer + 1], axis=0, keepdims=True) if lb_layer else jnp.zeros((1, W), F32)
    f = lb + (1.0 - lb) * jax.nn.sigmoid(c_f)
    lf = jnp.log(f)
    kk = 1.0 - f
    b = lf
    s = 1
    while s < SUBLANES:
        b = b + jnp.where(sub >= s, _group_roll(b, s), 0.0)
        s *= 2
    b_g = b.reshape(n_groups, SUBLANES, W)
    b_parts = []
    for g in range(n_groups):
        if g % (HGRN_CHUNK // SUBLANES) == 0:
            b_parts.append(b_g[g])
        else:
            b_parts.append(b_g[g] + b_parts[-1][SUBLANES - 1:SUBLANES, :])
    b = jnp.concatenate(b_parts, axis=0)
    fill()
    b3 = b.reshape(n_chunks, HGRN_CHUNK, W)
    b_last = b3[:, HGRN_CHUNK - 1:HGRN_CHUNK, :]
    qe_b = (c_q * jnp.exp(b)).astype(BF16)
    kd_b = (kk.reshape(n_chunks, HGRN_CHUNK, W) * jnp.exp(b_last - b3)).reshape(tt, W).astype(BF16)
    decay_chunk = jnp.exp(b_last)
    v_b = v.astype(BF16)

    sq_r = lax.broadcasted_iota(jnp.int32, (W, W), 0)
    sq_c = lax.broadcasted_iota(jnp.int32, (W, W), 1)
    same_head = (sq_r ^ sq_c) < HGRN_DK
    head_ones = jnp.where(same_head, 1.0, 0.0).astype(BF16)

    pair_xor = (lax.broadcasted_iota(jnp.int32, (HGRN_CHUNK, W), 0)
                ^ (lax.broadcasted_iota(jnp.int32, (HGRN_CHUNK, W), 1) & (HGRN_CHUNK - 1)))
    levels = []
    half = SUBLANES
    while half < HGRN_CHUNK:
        blk = 2 * half
        upper = (row & half) != 0
        bb = b.reshape(tt // blk, blk, W)
        delta = (bb - bb[:, half - 1:half, :]).reshape(tt, W)
        q_l = jnp.where(upper, c_q * jnp.exp(jnp.minimum(delta, 0.0)), 0.0)
        k_l = jnp.where(upper, 0.0, kk * jnp.exp(jnp.minimum(-delta, 0.0)))
        levels.append((q_l.astype(BF16), k_l.astype(BF16), blk))
        half = blk

    chunk_rows = [slice(c * HGRN_CHUNK, (c + 1) * HGRN_CHUNK) for c in range(n_chunks)]

    def head_stack(x_chunk):
        return jnp.where(same_head, jnp.concatenate([x_chunk] * HGRN_HEADS, axis=0), 0.0)

    updates = []
    for c, sl in enumerate(chunk_rows):
        updates.append(_dot_tn(v_b[sl], kd_b[sl]))
        if c % 2 == 1:
            fill()
    probs = []
    for c, sl in enumerate(chunk_rows):
        scores = None
        for q_l, k_l, xor_bound in reversed(levels):
            level_scores = _dot_nt(q_l[sl], head_stack(k_l[sl]))
            scores = level_scores if scores is None else jnp.where(pair_xor < xor_bound, level_scores, scores)
        probs.append(jnp.where(pair_xor < SUBLANES, 0.0, scores).astype(BF16))
        fill()
    st = st_ref[...]
    o_parts = []
    for c, sl in enumerate(chunk_rows):
        o_c = _dot_nt(qe_b[sl], st.astype(BF16))
        o_parts.append(o_c + _dot(probs[c], head_stack(v_b[sl])))
        st = st * decay_chunk[c] + jnp.where(same_head, updates[c], 0.0)
    st_ref[...] = st
    o = jnp.concatenate(o_parts, axis=0)

    decay = None
    f_d = f
    for d in range(SUBLANES):
        if d == 0:
            w_d, v_d = c_q * kk, v
        else:
            decay = f if d == 1 else decay * f_d
            f_d = _group_roll(f, d)
            w_d = jnp.where(sub >= d, c_q * (1.0 - f_d) * decay, 0.0)
            v_d = _group_roll(v, d)
        o = o + _dot(w_d.astype(BF16), head_ones) * v_d
        fill()

    ms = _dot((o * o).astype(BF16), head_ones) * (1.0 / HGRN_DK)
    y_c = o * lax.rsqrt(ms + EPS) * hnorm_ref[...] * jax.nn.silu(proj(6))
    merged = merge(merged, 2, y_c)

    s_b = proj(7)
    z = proj(8) * proj(9)
    ext = jnp.concatenate([tail_ref[2], z], axis=0)
    tail_ref[2] = z[tt - TAIL:]
    y_d = s_b * sum(sconvw_ref[k:k + 1, :] * _shift_rows(ext, SCONV_WIDTH - 1 - k, tt)
                    for k in range(SCONV_WIDTH))

    merged = merge(merged, 3, y_d)
    merged_b = merged.astype(BF16)
    for rows in row_blocks:
        mix = _dot(merged_b[rows], wout_ref[...])
        out_ref[0, rows, :] = h_ref[0, rows, :] + _rms(mix, gpost_ref[...])


def _mlp_kernel(h_ref, gpre_ref, gpost_ref, wup_ref, wdown_ref, out_ref):
    n_rows = h_ref.shape[0]
    for i in range(MLP_ROW_BLOCKS):
        rows = slice(i * n_rows // MLP_ROW_BLOCKS, (i + 1) * n_rows // MLP_ROW_BLOCKS)
        h_in = h_ref[rows, :]
        u = _rms(h_in, gpre_ref[...]).astype(BF16)
        m = jnp.zeros(h_in.shape, F32)
        for j in range(D_FF // FF_CHUNK):
            cols = slice(j * FF_CHUNK, (j + 1) * FF_CHUNK)
            hid = jnp.square(jnp.maximum(_dot(u, wup_ref[:, cols]), 0.0))
            m = m + _dot(hid.astype(BF16), wdown_ref[cols, :])
        out_ref[rows, :] = h_in + _rms(m, gpost_ref[...])


def _whole(arr):
    nd = arr.ndim
    return pl.BlockSpec(arr.shape, lambda *_: (0,) * nd)


def _one_layer(arr, layer):
    nd = arr.ndim
    return pl.BlockSpec((None,) + arr.shape[1:], lambda *_: (layer,) + (0,) * (nd - 1),
                        pipeline_mode=pl.Buffered(1))


def _block_diag(blocks):
    n, bi, bo = blocks.shape
    eye = jnp.eye(n, dtype=blocks.dtype)
    return jnp.einsum("nio,nm->nimo", blocks, eye).reshape(n * bi, n * bo)


def _mixer_layer(h, layer, params):
    bsz, seq, _ = h.shape
    row2 = lambda p: p.reshape(1, -1).astype(F32)
    operands = [
        h,
        row2(params["norm_mix_pre"]), row2(params["norm_mix_post"]),
        params["w_in_all"],
        params["lru_conv_w"].astype(F32), row2(params["lru_conv_b"]),
        _block_diag(params["lru_w_a"]).astype(BF16), row2(params["lru_b_a"]),
        _block_diag(params["lru_w_x"]).astype(BF16), row2(params["lru_b_x"]),
        row2(params["lru_lambda"]),
        _block_diag(params["pool_w"]).astype(BF16), row2(params["pool_scale"]),
        params["hgrn_lower_bound_all"].astype(F32),
        row2(jnp.tile(params["hgrn_norm"], HGRN_HEADS)),
        params["sconv_w"].astype(F32),
        params["w_branch_all"],
        params["w_out_all"],
    ]
    stacked = (3, len(operands) - 2, len(operands) - 1)
    in_specs = [pl.BlockSpec((1, TIME_TILE, D_MODEL), lambda bi, ti: (bi, ti, 0))]
    in_specs += [_one_layer(op, layer) if i in stacked else _whole(op)
                 for i, op in enumerate(operands) if i > 0]
    return pl.pallas_call(
        functools.partial(_mixer_kernel, layer),
        grid=(bsz, seq // TIME_TILE),
        in_specs=in_specs,
        out_specs=pl.BlockSpec((1, TIME_TILE, D_MODEL), lambda bi, ti: (bi, ti, 0)),
        out_shape=jax.ShapeDtypeStruct(h.shape, h.dtype),
        scratch_shapes=[
            pltpu.VMEM((3, TAIL, W), F32),
            pltpu.VMEM((8, W), F32),
            pltpu.VMEM((W, W), F32),
        ],
        compiler_params=pltpu.CompilerParams(
            dimension_semantics=("parallel", "arbitrary"),
            vmem_limit_bytes=VMEM_LIMIT_BYTES),
        name=f"mixer_l{layer}",
    )(*operands)


def _mlp_layer(h, layer, params):
    bsz, seq, d = h.shape
    h2 = h.reshape(bsz * seq, d)
    row2 = lambda p: p.reshape(1, -1).astype(F32)
    operands = [h2, row2(params["norm_mlp_pre"]), row2(params["norm_mlp_post"]),
                params["w_up_all"], params["w_down_all"]]
    in_specs = [pl.BlockSpec((MLP_TILE, d), lambda i: (i, 0))] + [_whole(op) for op in operands[1:3]]
    in_specs += [_one_layer(op, layer) for op in operands[3:]]
    out = pl.pallas_call(
        _mlp_kernel,
        grid=(bsz * seq // MLP_TILE,),
        in_specs=in_specs,
        out_specs=pl.BlockSpec((MLP_TILE, d), lambda i: (i, 0)),
        out_shape=jax.ShapeDtypeStruct(h2.shape, h2.dtype),
        compiler_params=pltpu.CompilerParams(
            dimension_semantics=("parallel",),
            vmem_limit_bytes=VMEM_LIMIT_BYTES),
        name=f"mlp_l{layer}",
    )(*operands)
    return out.reshape(bsz, seq, d)


def kernel(x, norm_mix_pre, norm_mix_post, norm_mlp_pre, norm_mlp_post, w_in, lru_conv_w, lru_conv_b,
           lru_w_a, lru_b_a, lru_w_x, lru_b_x, lru_lambda, pool_w, pool_scale, hgrn_lower_bound,
           hgrn_norm, sconv_w, w_branch, w_out, w_up, w_down):
    per_layer = dict(
        norm_mix_pre=norm_mix_pre, norm_mix_post=norm_mix_post, norm_mlp_pre=norm_mlp_pre,
        norm_mlp_post=norm_mlp_post, lru_conv_w=lru_conv_w, lru_conv_b=lru_conv_b,
        lru_w_a=lru_w_a, lru_b_a=lru_b_a, lru_w_x=lru_w_x, lru_b_x=lru_b_x, lru_lambda=lru_lambda,
        pool_w=pool_w, pool_scale=pool_scale, hgrn_norm=hgrn_norm, sconv_w=sconv_w)
    all_layers = dict(
        hgrn_lower_bound_all=hgrn_lower_bound, w_in_all=w_in.astype(BF16), w_branch_all=w_branch.astype(BF16),
        w_out_all=w_out.astype(BF16), w_up_all=w_up.astype(BF16), w_down_all=w_down.astype(BF16))
    depth = w_in.shape[0]
    h = x
    for layer in range(depth):
        params = {name: p[layer] for name, p in per_layer.items()}
        params.update(all_layers)
        h = _mixer_layer(h, layer, params)
        h = _mlp_layer(h, layer, params)
    return h
```

```python
import functools

import jax
import jax.numpy as jnp
from jax import lax
from jax.experimental import pallas as pl
from jax.experimental.pallas import tpu as pltpu

D_MODEL = 1024
N_BRANCHES = 4
W = D_MODEL // N_BRANCHES
LRU_CONV = 4
LRU_C = 8.0
POOL_WINDOWS = (2, 4, 8, 16)
POOL_GROUP = W // len(POOL_WINDOWS)
HGRN_HEADS = 4
HGRN_DK = W // HGRN_HEADS
SCONV_WIDTH = 3
D_FF = 4 * D_MODEL
N_MIX_SLOTS = 10
MIX_WIDTH = N_MIX_SLOTS * W
EPS = 1e-6

TIME_TILE = 512
HGRN_CHUNK = 64
SUBLANES = 8
EDGE_BLOCKS = 4
TAIL = 16
MLP_TILE = 512
MLP_ROW_BLOCKS = 2
FF_CHUNK = 1024
VMEM_LIMIT_BYTES = 56 * 1024 * 1024

F32 = jnp.float32
BF16 = jnp.bfloat16


def _dot(a, b):
    return jnp.dot(a, b, preferred_element_type=F32)


def _dot_nt(a, b):
    return lax.dot_general(a, b, (((1,), (1,)), ((), ())), preferred_element_type=F32)


def _dot_tn(a, b):
    return lax.dot_general(a, b, (((0,), (0,)), ((), ())), preferred_element_type=F32)


def _rms(x, gain):
    return x * lax.rsqrt(jnp.mean(x * x, axis=-1, keepdims=True) + EPS) * gain


def _group_roll(x, s):
    n, w = x.shape
    return pltpu.roll(x.reshape(n // SUBLANES, SUBLANES, w), s, 1).reshape(n, w)


def _shift_rows(ext, s, n):
    if s == 0:
        return ext[TAIL:TAIL + n]
    return pltpu.roll(ext, s, 0)[TAIL:TAIL + n]


def _mixer_kernel(lb_layer, h_ref, gpre_ref, gpost_ref, win_ref, convw_ref, convb_ref, wa_ref, ba_ref,
                  wx_ref, bx_ref, lam_ref, poolw_ref, pools_ref, lbraw_ref, hnorm_ref, sconvw_ref,
                  wbr_ref, wout_ref, out_ref, tail_ref, hlru_ref, st_ref):
    tt = h_ref.shape[1]
    n_chunks = tt // HGRN_CHUNK

    @pl.when(pl.program_id(1) == 0)
    def _():
        tail_ref[...] = jnp.zeros_like(tail_ref)
        hlru_ref[...] = jnp.zeros_like(hlru_ref)
        st_ref[...] = jnp.zeros_like(st_ref)

    row_blocks = [slice(i * tt // EDGE_BLOCKS, (i + 1) * tt // EDGE_BLOCKS) for i in range(EDGE_BLOCKS)]
    u_blocks = [_rms(h_ref[0, rows, :], gpre_ref[...]).astype(BF16) for rows in row_blocks]
    u = jnp.concatenate(u_blocks, axis=0)

    row = lax.broadcasted_iota(jnp.int32, (tt, W), 0)
    lane = lax.broadcasted_iota(jnp.int32, (tt, W), 1)
    sub = row & (SUBLANES - 1)
    n_groups = tt // SUBLANES

    gate_cols = D_MODEL // W
    gate_keys = lambda k: [("gate", k, j) for j in range(gate_cols)]
    pending = (gate_keys(0) + [("slot", 1)] + gate_keys(1) + gate_keys(2)
               + [("slot", s) for s in (6, 7, 8, 9)] + gate_keys(3))
    early = {}

    def issue(key):
        pending.remove(key)
        if key[0] == "slot":
            early[key] = _dot(u, win_ref[:, key[1] * W:(key[1] + 1) * W])
        else:
            lo = MIX_WIDTH + key[1] * D_MODEL + key[2] * W
            early[key] = jax.nn.sigmoid(_dot(u, win_ref[:, lo:lo + W]))

    def fill():
        if pending:
            issue(pending[0])

    def take(key):
        if key in pending:
            issue(key)
        return early.pop(key)

    def proj(slot):
        if slot == 0:
            w_slot = win_ref[:, 0:W]
            return jnp.concatenate([_dot(u_blk, w_slot) for u_blk in u_blocks], axis=0)
        if ("slot", slot) in pending or ("slot", slot) in early:
            return take(("slot", slot))
        return _dot(u, win_ref[:, slot * W:(slot + 1) * W])

    def merge(acc, k, y_k):
        gate = jnp.concatenate([take(key) for key in gate_keys(k)], axis=1)
        return acc + gate * _dot(y_k.astype(BF16), wbr_ref[k])

    a_x = proj(0)
    ext = jnp.concatenate([tail_ref[0], a_x], axis=0)
    tail_ref[0] = a_x[tt - TAIL:]
    xc = convb_ref[...] + sum(convw_ref[k:k + 1, :] * _shift_rows(ext, LRU_CONV - 1 - k, tt)
                              for k in range(LRU_CONV))
    fill()
    xc_b = xc.astype(BF16)
    r = jax.nn.sigmoid(_dot(xc_b, wa_ref[...]) + ba_ref[...])
    i_gate = jax.nn.sigmoid(_dot(xc_b, wx_ref[...]) + bx_ref[...])
    log_a = (-LRU_C) * r * jax.nn.softplus(-lam_ref[...])
    a = jnp.exp(log_a)
    mult = jnp.sqrt(jnp.tanh(-log_a) * (1.0 + a * a))
    uu = mult * i_gate * xc
    fill()
    s = 1
    while s < SUBLANES:
        keep = sub >= s
        a_sh = jnp.where(keep, _group_roll(a, s), 1.0)
        u_sh = jnp.where(keep, _group_roll(uu, s), 0.0)
        uu = uu + a * u_sh
        a = a * a_sh
        s *= 2
    fill()
    a_g = a.reshape(n_groups, SUBLANES, W)
    u_g = uu.reshape(n_groups, SUBLANES, W)
    h_prev = hlru_ref[0:1, :]
    h_groups = []
    for g in range(n_groups):
        h_g = u_g[g] + a_g[g] * h_prev
        h_groups.append(h_g)
        h_prev = h_g[SUBLANES - 1:SUBLANES, :]
    hlru_ref[0:1, :] = h_prev
    h_seq = jnp.concatenate(h_groups, axis=0)
    y_a = h_seq * jax.nn.gelu(proj(1))
    merged = merge(jnp.zeros((tt, D_MODEL), F32), 0, y_a)

    p_x = proj(2)
    ext = jnp.concatenate([tail_ref[1], p_x], axis=0)
    tail_ref[1] = p_x[tt - TAIL:]
    group = lax.shift_right_logical(lane, POOL_GROUP.bit_length() - 1)
    win_sum = ext
    mean = jnp.zeros((tt, W), F32)
    pos = (pl.program_id(1) * tt + row + 1).astype(F32)
    for g, win in enumerate(POOL_WINDOWS):
        win_sum = win_sum + pltpu.roll(win_sum, win // 2, 0)
        mean = jnp.where(group == g, win_sum[TAIL:] / jnp.minimum(pos, float(win)), mean)
    fill()
    y_b = _dot((mean - p_x).astype(BF16), poolw_ref[...]) * pools_ref[...]
    merged = merge(merged, 1, y_b)

    c_q = proj(3)
    c_f = proj(4)
    v = proj(5)
    lb_sm = jax.nn.softmax(lbraw_ref[...], axis=0)
    lb = jnp.sum(lb_sm[1:lb_layer + 1], axis=0, keepdims=True) if lb_layer else jnp.zeros((1, W), F32)
    f = lb + (1.0 - lb) * jax.nn.sigmoid(c_f)
    lf = jnp.log(f)
    kk = 1.0 - f
    b = lf
    s = 1
    while s < SUBLANES:
        b = b + jnp.where(sub >= s, _group_roll(b, s), 0.0)
        s *= 2
    b_g = b.reshape(n_groups, SUBLANES, W)
    b_parts = []
    for g in range(n_groups):
        if g % (HGRN_CHUNK // SUBLANES) == 0:
            b_parts.append(b_g[g])
        else:
            b_parts.append(b_g[g] + b_parts[-1][SUBLANES - 1:SUBLANES, :])
    b = jnp.concatenate(b_parts, axis=0)
    fill()
    b3 = b.reshape(n_chunks, HGRN_CHUNK, W)
    b_last = b3[:, HGRN_CHUNK - 1:HGRN_CHUNK, :]
    qe_b = (c_q * jnp.exp(b)).astype(BF16)
    kd_b = (kk.reshape(n_chunks, HGRN_CHUNK, W) * jnp.exp(b_last - b3)).reshape(tt, W).astype(BF16)
    decay_chunk = jnp.exp(b_last)
    v_b = v.astype(BF16)

    sq_r = lax.broadcasted_iota(jnp.int32, (W, W), 0)
    sq_c = lax.broadcasted_iota(jnp.int32, (W, W), 1)
    same_head = (sq_r ^ sq_c) < HGRN_DK
    head_ones = jnp.where(same_head, 1.0, 0.0).astype(BF16)

    pair_xor = (lax.broadcasted_iota(jnp.int32, (HGRN_CHUNK, W), 0)
                ^ (lax.broadcasted_iota(jnp.int32, (HGRN_CHUNK, W), 1) & (HGRN_CHUNK - 1)))
    levels = []
    half = SUBLANES
    while half < HGRN_CHUNK:
        blk = 2 * half
        upper = (row & half) != 0
        bb = b.reshape(tt // blk, blk, W)
        delta = (bb - bb[:, half - 1:half, :]).reshape(tt, W)
        q_l = jnp.where(upper, c_q * jnp.exp(jnp.minimum(delta, 0.0)), 0.0)
        k_l = jnp.where(upper, 0.0, kk * jnp.exp(jnp.minimum(-delta, 0.0)))
        levels.append((q_l.astype(BF16), k_l.astype(BF16), blk))
        half = blk

    chunk_rows = [slice(c * HGRN_CHUNK, (c + 1) * HGRN_CHUNK) for c in range(n_chunks)]

    def head_stack(x_chunk):
        return jnp.where(same_head, jnp.concatenate([x_chunk] * HGRN_HEADS, axis=0), 0.0)

    updates = []
    for c, sl in enumerate(chunk_rows):
        updates.append(_dot_tn(v_b[sl], kd_b[sl]))
        if c % 2 == 1:
            fill()
    probs = []
    n_small = 0
    for c, sl in enumerate(chunk_rows):
        scores = None
        for q_l, k_l, xor_bound in reversed(levels):
            level_scores = _dot_nt(q_l[sl], head_stack(k_l[sl]))
            scores = level_scores if scores is None else jnp.where(pair_xor < xor_bound, level_scores, scores)
            n_small += 1
            if n_small % 2 == 0:
                fill()
        probs.append(jnp.where(pair_xor < SUBLANES, 0.0, scores).astype(BF16))

    assert n_chunks == SUBLANES
    st = st_ref[...]
    o_parts = []
    o_same = None
    decay = None
    f_d = f
    for d, sl in enumerate(chunk_rows):
        o_c = _dot_nt(qe_b[sl], st.astype(BF16))
        o_parts.append(o_c + _dot(probs[d], head_stack(v_b[sl])))
        st = st * decay_chunk[d] + jnp.where(same_head, updates[d], 0.0)
        if d == 0:
            w_d, v_d = c_q * kk, v
        else:
            decay = f if d == 1 else decay * f_d
            f_d = _group_roll(f, d)
            w_d = jnp.where(sub >= d, c_q * (1.0 - f_d) * decay, 0.0)
            v_d = _group_roll(v, d)
        term = _dot(w_d.astype(BF16), head_ones) * v_d
        o_same = term if o_same is None else o_same + term
    st_ref[...] = st
    o = jnp.concatenate(o_parts, axis=0) + o_same

    ms = _dot((o * o).astype(BF16), head_ones) * (1.0 / HGRN_DK)
    y_c = o * lax.rsqrt(ms + EPS) * hnorm_ref[...] * jax.nn.silu(proj(6))
    merged = merge(merged, 2, y_c)

    s_b = proj(7)
    z = proj(8) * proj(9)
    ext = jnp.concatenate([tail_ref[2], z], axis=0)
    tail_ref[2] = z[tt - TAIL:]
    y_d = s_b * sum(sconvw_ref[k:k + 1, :] * _shift_rows(ext, SCONV_WIDTH - 1 - k, tt)
                    for k in range(SCONV_WIDTH))

    merged = merge(merged, 3, y_d)
    merged_b = merged.astype(BF16)
    for rows in row_blocks:
        mix = _dot(merged_b[rows], wout_ref[...])
        out_ref[0, rows, :] = h_ref[0, rows, :] + _rms(mix, gpost_ref[...])


def _mlp_kernel(h_ref, gpre_ref, gpost_ref, wup_ref, wdown_ref, out_ref):
    n_rows = h_ref.shape[0]
    for i in range(MLP_ROW_BLOCKS):
        rows = slice(i * n_rows // MLP_ROW_BLOCKS, (i + 1) * n_rows // MLP_ROW_BLOCKS)
        h_in = h_ref[rows, :]
        u = _rms(h_in, gpre_ref[...]).astype(BF16)
        m = jnp.zeros(h_in.shape, F32)
        for j in range(D_FF // FF_CHUNK):
            cols = slice(j * FF_CHUNK, (j + 1) * FF_CHUNK)
            hid = jnp.square(jnp.maximum(_dot(u, wup_ref[:, cols]), 0.0))
            m = m + _dot(hid.astype(BF16), wdown_ref[cols, :])
        out_ref[rows, :] = h_in + _rms(m, gpost_ref[...])


def _whole(arr):
    nd = arr.ndim
    return pl.BlockSpec(arr.shape, lambda *_: (0,) * nd)


def _one_layer(arr, layer):
    nd = arr.ndim
    return pl.BlockSpec((None,) + arr.shape[1:], lambda *_: (layer,) + (0,) * (nd - 1),
                        pipeline_mode=pl.Buffered(1))


def _block_diag(blocks):
    n, bi, bo = blocks.shape
    eye = jnp.eye(n, dtype=blocks.dtype)
    return jnp.einsum("nio,nm->nimo", blocks, eye).reshape(n * bi, n * bo)


def _mixer_layer(h, layer, params):
    bsz, seq, _ = h.shape
    row2 = lambda p: p.reshape(1, -1).astype(F32)
    operands = [
        h,
        row2(params["norm_mix_pre"]), row2(params["norm_mix_post"]),
        params["w_in_all"],
        params["lru_conv_w"].astype(F32), row2(params["lru_conv_b"]),
        _block_diag(params["lru_w_a"]).astype(BF16), row2(params["lru_b_a"]),
        _block_diag(params["lru_w_x"]).astype(BF16), row2(params["lru_b_x"]),
        row2(params["lru_lambda"]),
        _block_diag(params["pool_w"]).astype(BF16), row2(params["pool_scale"]),
        params["hgrn_lower_bound_all"].astype(F32),
        row2(jnp.tile(params["hgrn_norm"], HGRN_HEADS)),
        params["sconv_w"].astype(F32),
        params["w_branch_all"],
        params["w_out_all"],
    ]
    stacked = (3, len(operands) - 2, len(operands) - 1)
    in_specs = [pl.BlockSpec((1, TIME_TILE, D_MODEL), lambda bi, ti: (bi, ti, 0))]
    in_specs += [_one_layer(op, layer) if i in stacked else _whole(op)
                 for i, op in enumerate(operands) if i > 0]
    return pl.pallas_call(
        functools.partial(_mixer_kernel, layer),
        grid=(bsz, seq // TIME_TILE),
        in_specs=in_specs,
        out_specs=pl.BlockSpec((1, TIME_TILE, D_MODEL), lambda bi, ti: (bi, ti, 0)),
        out_shape=jax.ShapeDtypeStruct(h.shape, h.dtype),
        scratch_shapes=[
            pltpu.VMEM((3, TAIL, W), F32),
            pltpu.VMEM((8, W), F32),
            pltpu.VMEM((W, W), F32),
        ],
        compiler_params=pltpu.CompilerParams(
            dimension_semantics=("parallel", "arbitrary"),
            vmem_limit_bytes=VMEM_LIMIT_BYTES),
        name=f"mixer_l{layer}",
    )(*operands)


def _mlp_layer(h, layer, params):
    bsz, seq, d = h.shape
    h2 = h.reshape(bsz * seq, d)
    row2 = lambda p: p.reshape(1, -1).astype(F32)
    operands = [h2, row2(params["norm_mlp_pre"]), row2(params["norm_mlp_post"]),
                params["w_up_all"], params["w_down_all"]]
    in_specs = [pl.BlockSpec((MLP_TILE, d), lambda i: (i, 0))] + [_whole(op) for op in operands[1:3]]
    in_specs += [_one_layer(op, layer) for op in operands[3:]]
    out = pl.pallas_call(
        _mlp_kernel,
        grid=(bsz * seq // MLP_TILE,),
        in_specs=in_specs,
        out_specs=pl.BlockSpec((MLP_TILE, d), lambda i: (i, 0)),
        out_shape=jax.ShapeDtypeStruct(h2.shape, h2.dtype),
        compiler_params=pltpu.CompilerParams(
            dimension_semantics=("parallel",),
            vmem_limit_bytes=VMEM_LIMIT_BYTES),
        name=f"mlp_l{layer}",
    )(*operands)
    return out.reshape(bsz, seq, d)


def kernel(x, norm_mix_pre, norm_mix_post, norm_mlp_pre, norm_mlp_post, w_in, lru_conv_w, lru_conv_b,
           lru_w_a, lru_b_a, lru_w_x, lru_b_x, lru_lambda, pool_w, pool_scale, hgrn_lower_bound,
           hgrn_norm, sconv_w, w_branch, w_out, w_up, w_down):
    per_layer = dict(
        norm_mix_pre=norm_mix_pre, norm_mix_post=norm_mix_post, norm_mlp_pre=norm_mlp_pre,
        norm_mlp_post=norm_mlp_post, lru_conv_w=lru_conv_w, lru_conv_b=lru_conv_b,
        lru_w_a=lru_w_a, lru_b_a=lru_b_a, lru_w_x=lru_w_x, lru_b_x=lru_b_x, lru_lambda=lru_lambda,
        pool_w=pool_w, pool_scale=pool_scale, hgrn_norm=hgrn_norm, sconv_w=sconv_w)
    all_layers = dict(
        hgrn_lower_bound_all=hgrn_lower_bound, w_in_all=w_in.astype(BF16), w_branch_all=w_branch.astype(BF16),
        w_out_all=w_out.astype(BF16), w_up_all=w_up.astype(BF16), w_down_all=w_down.astype(BF16))
    depth = w_in.shape[0]
    h = x
    for layer in range(depth):
        params = {name: p[layer] for name, p in per_layer.items()}
        params.update(all_layers)
        h = _mixer_layer(h, layer, params)
        h = _mlp_layer(h, layer, params)
    return h
```

```python
import functools

import jax
import jax.numpy as jnp
from jax import lax
from jax.experimental import pallas as pl
from jax.experimental.pallas import tpu as pltpu

D_MODEL = 1024
N_BRANCHES = 4
W = D_MODEL // N_BRANCHES
LRU_CONV = 4
LRU_C = 8.0
POOL_WINDOWS = (2, 4, 8, 16)
POOL_GROUP = W // len(POOL_WINDOWS)
HGRN_HEADS = 4
HGRN_DK = W // HGRN_HEADS
SCONV_WIDTH = 3
D_FF = 4 * D_MODEL
N_MIX_SLOTS = 10
MIX_WIDTH = N_MIX_SLOTS * W
EPS = 1e-6

TIME_TILE = 512
HGRN_CHUNK = 64
SUBLANES = 8
EDGE_BLOCKS = 4
TAIL = 16
MLP_TILE = 1024
MLP_ROW_BLOCKS = 2
FF_CHUNK = 1024
VMEM_LIMIT_BYTES = 56 * 1024 * 1024

F32 = jnp.float32
BF16 = jnp.bfloat16


def _dot(a, b):
    return jnp.dot(a, b, preferred_element_type=F32)


def _dot_nt(a, b):
    return lax.dot_general(a, b, (((1,), (1,)), ((), ())), preferred_element_type=F32)


def _dot_tn(a, b):
    return lax.dot_general(a, b, (((0,), (0,)), ((), ())), preferred_element_type=F32)


def _rms(x, gain):
    return x * lax.rsqrt(jnp.mean(x * x, axis=-1, keepdims=True) + EPS) * gain


def _group_roll(x, s):
    n, w = x.shape
    return pltpu.roll(x.reshape(n // SUBLANES, SUBLANES, w), s, 1).reshape(n, w)


def _shift_rows(ext, s, n):
    if s == 0:
        return ext[TAIL:TAIL + n]
    return pltpu.roll(ext, s, 0)[TAIL:TAIL + n]


def _cast_slabs(refs):
    n = len(refs) // 2
    for src_ref, dst_ref in zip(refs[:n], refs[n:]):
        dst_ref[...] = src_ref[...].astype(BF16)


def _mixer_kernel(lb_layer, n_cast, h_ref, gpre_ref, gpost_ref, win_ref, convw_ref, convb_ref, wa_ref,
                  ba_ref, wx_ref, bx_ref, lam_ref, poolw_ref, pools_ref, lbraw_ref, hnorm_ref, sconvw_ref,
                  wbr_ref, wout_ref, *rest):
    cast_srcs, out_ref, cast_dsts = rest[:n_cast], rest[n_cast], rest[n_cast + 1:2 * n_cast + 1]
    tail_ref, hlru_ref, st_ref = rest[2 * n_cast + 1:]
    _cast_slabs(cast_srcs + cast_dsts)
    tt = h_ref.shape[1]
    n_chunks = tt // HGRN_CHUNK

    @pl.when(pl.program_id(1) == 0)
    def _():
        tail_ref[...] = jnp.zeros_like(tail_ref)
        hlru_ref[...] = jnp.zeros_like(hlru_ref)
        st_ref[...] = jnp.zeros_like(st_ref)

    row_blocks = [slice(i * tt // EDGE_BLOCKS, (i + 1) * tt // EDGE_BLOCKS) for i in range(EDGE_BLOCKS)]
    u_blocks = [_rms(h_ref[0, rows, :], gpre_ref[...]).astype(BF16) for rows in row_blocks]
    u = jnp.concatenate(u_blocks, axis=0)

    row = lax.broadcasted_iota(jnp.int32, (tt, W), 0)
    lane = lax.broadcasted_iota(jnp.int32, (tt, W), 1)
    sub = row & (SUBLANES - 1)
    n_groups = tt // SUBLANES

    gate_cols = D_MODEL // W
    gate_keys = lambda k: [("gate", k, j) for j in range(gate_cols)]
    pending = (gate_keys(0) + [("slot", 1)] + gate_keys(1) + gate_keys(2)
               + [("slot", s) for s in (6, 7, 8, 9)] + gate_keys(3))
    early = {}

    def issue(key):
        pending.remove(key)
        if key[0] == "slot":
            early[key] = _dot(u, win_ref[:, key[1] * W:(key[1] + 1) * W])
        else:
            lo = MIX_WIDTH + key[1] * D_MODEL + key[2] * W
            early[key] = jax.nn.sigmoid(_dot(u, win_ref[:, lo:lo + W]))

    def fill():
        if pending:
            issue(pending[0])

    def take(key):
        if key in pending:
            issue(key)
        return early.pop(key)

    def proj(slot):
        if slot == 0:
            w_slot = win_ref[:, 0:W]
            return jnp.concatenate([_dot(u_blk, w_slot) for u_blk in u_blocks], axis=0)
        if ("slot", slot) in pending or ("slot", slot) in early:
            return take(("slot", slot))
        return _dot(u, win_ref[:, slot * W:(slot + 1) * W])

    def merge(acc, k, y_k):
        gate = jnp.concatenate([take(key) for key in gate_keys(k)], axis=1)
        return acc + gate * _dot(y_k.astype(BF16), wbr_ref[k * W:(k + 1) * W, :])

    a_x = proj(0)
    ext = jnp.concatenate([tail_ref[0], a_x], axis=0)
    tail_ref[0] = a_x[tt - TAIL:]
    xc = convb_ref[...] + sum(convw_ref[k:k + 1, :] * _shift_rows(ext, LRU_CONV - 1 - k, tt)
                              for k in range(LRU_CONV))
    fill()
    xc_b = xc.astype(BF16)
    r = jax.nn.sigmoid(_dot(xc_b, wa_ref[...]) + ba_ref[...])
    i_gate = jax.nn.sigmoid(_dot(xc_b, wx_ref[...]) + bx_ref[...])
    log_a = (-LRU_C) * r * jax.nn.softplus(-lam_ref[...])
    a = jnp.exp(log_a)
    mult = jnp.sqrt(jnp.tanh(-log_a) * (1.0 + a * a))
    uu = mult * i_gate * xc
    fill()
    s = 1
    while s < SUBLANES:
        keep = sub >= s
        a_sh = jnp.where(keep, _group_roll(a, s), 1.0)
        u_sh = jnp.where(keep, _group_roll(uu, s), 0.0)
        uu = uu + a * u_sh
        a = a * a_sh
        s *= 2
    fill()
    a_g = a.reshape(n_groups, SUBLANES, W)
    u_g = uu.reshape(n_groups, SUBLANES, W)
    h_prev = hlru_ref[0:1, :]
    h_groups = []
    for g in range(n_groups):
        h_g = u_g[g] + a_g[g] * h_prev
        h_groups.append(h_g)
        h_prev = h_g[SUBLANES - 1:SUBLANES, :]
    hlru_ref[0:1, :] = h_prev
    h_seq = jnp.concatenate(h_groups, axis=0)
    y_a = h_seq * jax.nn.gelu(proj(1))
    merged = merge(jnp.zeros((tt, D_MODEL), F32), 0, y_a)

    p_x = proj(2)
    ext = jnp.concatenate([tail_ref[1], p_x], axis=0)
    tail_ref[1] = p_x[tt - TAIL:]
    group = lax.shift_right_logical(lane, POOL_GROUP.bit_length() - 1)
    win_sum = ext
    mean = jnp.zeros((tt, W), F32)
    pos = (pl.program_id(1) * tt + row + 1).astype(F32)
    for g, win in enumerate(POOL_WINDOWS):
        win_sum = win_sum + pltpu.roll(win_sum, win // 2, 0)
        mean = jnp.where(group == g, win_sum[TAIL:] / jnp.minimum(pos, float(win)), mean)
    fill()
    y_b = _dot((mean - p_x).astype(BF16), poolw_ref[...]) * pools_ref[...]
    merged = merge(merged, 1, y_b)

    c_q = proj(3)
    c_f = proj(4)
    v = proj(5)
    lb_sm = jax.nn.softmax(lbraw_ref[...], axis=0)
    lb = jnp.sum(lb_sm[1:lb_layer + 1], axis=0, keepdims=True) if lb_layer else jnp.zeros((1, W), F32)
    f = lb + (1.0 - lb) * jax.nn.sigmoid(c_f)
    lf = jnp.log(f)
    kk = 1.0 - f
    b = lf
    s = 1
    while s < SUBLANES:
        b = b + jnp.where(sub >= s, _group_roll(b, s), 0.0)
        s *= 2
    b_g = b.reshape(n_groups, SUBLANES, W)
    b_parts = []
    for g in range(n_groups):
        if g % (HGRN_CHUNK // SUBLANES) == 0:
            b_parts.append(b_g[g])
        else:
            b_parts.append(b_g[g] + b_parts[-1][SUBLANES - 1:SUBLANES, :])
    b = jnp.concatenate(b_parts, axis=0)
    fill()
    b3 = b.reshape(n_chunks, HGRN_CHUNK, W)
    b_last = b3[:, HGRN_CHUNK - 1:HGRN_CHUNK, :]
    qe_b = (c_q * jnp.exp(b)).astype(BF16)
    kd_b = (kk.reshape(n_chunks, HGRN_CHUNK, W) * jnp.exp(b_last - b3)).reshape(tt, W).astype(BF16)
    decay_chunk = jnp.exp(b_last)
    v_b = v.astype(BF16)

    sq_r = lax.broadcasted_iota(jnp.int32, (W, W), 0)
    sq_c = lax.broadcasted_iota(jnp.int32, (W, W), 1)
    same_head = (sq_r ^ sq_c) < HGRN_DK
    head_ones = jnp.where(same_head, 1.0, 0.0).astype(BF16)

    pair_xor = (lax.broadcasted_iota(jnp.int32, (HGRN_CHUNK, W), 0)
                ^ (lax.broadcasted_iota(jnp.int32, (HGRN_CHUNK, W), 1) & (HGRN_CHUNK - 1)))
    levels = []
    half = SUBLANES
    while half < HGRN_CHUNK:
        blk = 2 * half
        upper = (row & half) != 0
        bb = b.reshape(tt // blk, blk, W)
        delta = (bb - bb[:, half - 1:half, :]).reshape(tt, W)
        q_l = jnp.where(upper, c_q * jnp.exp(jnp.minimum(delta, 0.0)), 0.0)
        k_l = jnp.where(upper, 0.0, kk * jnp.exp(jnp.minimum(-delta, 0.0)))
        levels.append((q_l.astype(BF16), k_l.astype(BF16), blk))
        half = blk

    chunk_rows = [slice(c * HGRN_CHUNK, (c + 1) * HGRN_CHUNK) for c in range(n_chunks)]

    def head_stack(x_chunk):
        return jnp.where(same_head, jnp.concatenate([x_chunk] * HGRN_HEADS, axis=0), 0.0)

    updates = []
    for c, sl in enumerate(chunk_rows):
        updates.append(_dot_tn(v_b[sl], kd_b[sl]))
        if c % 2 == 1:
            fill()
    probs = []
    n_small = 0
    for c, sl in enumerate(chunk_rows):
        scores = None
        for q_l, k_l, xor_bound in reversed(levels):
            level_scores = _dot_nt(q_l[sl], head_stack(k_l[sl]))
            scores = level_scores if scores is None else jnp.where(pair_xor < xor_bound, level_scores, scores)
            n_small += 1
            if n_small % 2 == 0:
                fill()
        probs.append(jnp.where(pair_xor < SUBLANES, 0.0, scores).astype(BF16))

    assert n_chunks == SUBLANES
    st = st_ref[...]
    o_parts = []
    o_same = None
    decay = None
    f_d = f
    for d, sl in enumerate(chunk_rows):
        o_c = _dot_nt(qe_b[sl], st.astype(BF16))
        o_parts.append(o_c + _dot(probs[d], head_stack(v_b[sl])))
        st = st * decay_chunk[d] + jnp.where(same_head, updates[d], 0.0)
        if d == 0:
            w_d, v_d = c_q * kk, v
        else:
            decay = f if d == 1 else decay * f_d
            f_d = _group_roll(f, d)
            w_d = jnp.where(sub >= d, c_q * (1.0 - f_d) * decay, 0.0)
            v_d = _group_roll(v, d)
        term = _dot(w_d.astype(BF16), head_ones) * v_d
        o_same = term if o_same is None else o_same + term
    st_ref[...] = st
    o = jnp.concatenate(o_parts, axis=0) + o_same

    ms = _dot((o * o).astype(BF16), head_ones) * (1.0 / HGRN_DK)
    y_c = o * lax.rsqrt(ms + EPS) * hnorm_ref[...] * jax.nn.silu(proj(6))
    merged = merge(merged, 2, y_c)

    s_b = proj(7)
    z = proj(8) * proj(9)
    ext = jnp.concatenate([tail_ref[2], z], axis=0)
    tail_ref[2] = z[tt - TAIL:]
    y_d = s_b * sum(sconvw_ref[k:k + 1, :] * _shift_rows(ext, SCONV_WIDTH - 1 - k, tt)
                    for k in range(SCONV_WIDTH))

    merged = merge(merged, 3, y_d)
    merged_b = merged.astype(BF16)
    for rows in row_blocks:
        mix = _dot(merged_b[rows], wout_ref[...])
        out_ref[0, rows, :] = h_ref[0, rows, :] + _rms(mix, gpost_ref[...])


def _mlp_kernel(n_cast, h_ref, gpre_ref, gpost_ref, wup_ref, wdown_ref, *rest):
    cast_srcs, out_ref, cast_dsts = rest[:n_cast], rest[n_cast], rest[n_cast + 1:]
    _cast_slabs(cast_srcs + cast_dsts)
    n_rows = h_ref.shape[0]
    for i in range(MLP_ROW_BLOCKS):
        rows = slice(i * n_rows // MLP_ROW_BLOCKS, (i + 1) * n_rows // MLP_ROW_BLOCKS)
        h_in = h_ref[rows, :]
        u = _rms(h_in, gpre_ref[...]).astype(BF16)
        m = jnp.zeros(h_in.shape, F32)
        for j in range(D_FF // FF_CHUNK):
            cols = slice(j * FF_CHUNK, (j + 1) * FF_CHUNK)
            hid = jnp.square(jnp.maximum(_dot(u, wup_ref[:, cols]), 0.0))
            m = m + _dot(hid.astype(BF16), wdown_ref[cols, :])
        out_ref[rows, :] = h_in + _rms(m, gpost_ref[...])


def _whole(arr):
    nd = arr.ndim
    return pl.BlockSpec(arr.shape, lambda *_: (0,) * nd, pipeline_mode=pl.Buffered(1))


def _cast_specs(sources, layer, n_steps, step_of):
    in_specs, out_specs, out_shapes = [], [], []
    for src in sources:
        _, rows, cols = src.shape
        slab = rows // n_steps
        in_specs.append(pl.BlockSpec((None, slab, cols), lambda *g: (layer, step_of(*g), 0)))
        out_specs.append(pl.BlockSpec((slab, cols), lambda *g: (step_of(*g), 0)))
        out_shapes.append(jax.ShapeDtypeStruct((rows, cols), BF16))
    return in_specs, out_specs, out_shapes


def _block_diag(blocks):
    n, bi, bo = blocks.shape
    eye = jnp.eye(n, dtype=blocks.dtype)
    return jnp.einsum("nio,nm->nimo", blocks, eye).reshape(n * bi, n * bo)


def _mixer_layer(h, layer, params, weights, cast_sources):
    bsz, seq, _ = h.shape
    n_t = seq // TIME_TILE
    row2 = lambda p: p.reshape(1, -1).astype(F32)
    w_in_b, w_branch_b, w_out_b = weights
    operands = [
        h,
        row2(params["norm_mix_pre"]), row2(params["norm_mix_post"]),
        w_in_b,
        params["lru_conv_w"].astype(F32), row2(params["lru_conv_b"]),
        _block_diag(params["lru_w_a"]).astype(BF16), row2(params["lru_b_a"]),
        _block_diag(params["lru_w_x"]).astype(BF16), row2(params["lru_b_x"]),
        row2(params["lru_lambda"]),
        _block_diag(params["pool_w"]).astype(BF16), row2(params["pool_scale"]),
        params["hgrn_lower_bound_all"].astype(F32),
        row2(jnp.tile(params["hgrn_norm"], HGRN_HEADS)),
        params["sconv_w"].astype(F32),
        w_branch_b,
        w_out_b,
    ]
    cast_in, cast_out, cast_shapes = _cast_specs(cast_sources, layer, bsz * n_t, lambda bi, ti: bi * n_t + ti)
    in_specs = [pl.BlockSpec((1, TIME_TILE, D_MODEL), lambda bi, ti: (bi, ti, 0))]
    in_specs += [_whole(op) for op in operands[1:]] + cast_in
    h_out, *cast = pl.pallas_call(
        functools.partial(_mixer_kernel, layer, len(cast_sources)),
        grid=(bsz, n_t),
        in_specs=in_specs,
        out_specs=[pl.BlockSpec((1, TIME_TILE, D_MODEL), lambda bi, ti: (bi, ti, 0))] + cast_out,
        out_shape=[jax.ShapeDtypeStruct(h.shape, h.dtype)] + cast_shapes,
        scratch_shapes=[
            pltpu.VMEM((3, TAIL, W), F32),
            pltpu.VMEM((8, W), F32),
            pltpu.VMEM((W, W), F32),
        ],
        compiler_params=pltpu.CompilerParams(
            dimension_semantics=("parallel", "arbitrary"),
            vmem_limit_bytes=VMEM_LIMIT_BYTES),
        name=f"mixer_l{layer}",
    )(*operands, *cast_sources)
    return h_out, cast


def _mlp_layer(h, layer, params, weights, cast_sources, cast_layer):
    bsz, seq, d = h.shape
    h2 = h.reshape(bsz * seq, d)
    n_steps = bsz * seq // MLP_TILE
    row2 = lambda p: p.reshape(1, -1).astype(F32)
    operands = [h2, row2(params["norm_mlp_pre"]), row2(params["norm_mlp_post"]), *weights]
    cast_in, cast_out, cast_shapes = _cast_specs(cast_sources, cast_layer, n_steps, lambda i: i)
    in_specs = [pl.BlockSpec((MLP_TILE, d), lambda i: (i, 0))] + [_whole(op) for op in operands[1:]] + cast_in
    out, *cast = pl.pallas_call(
        functools.partial(_mlp_kernel, len(cast_sources)),
        grid=(n_steps,),
        in_specs=in_specs,
        out_specs=[pl.BlockSpec((MLP_TILE, d), lambda i: (i, 0))] + cast_out,
        out_shape=[jax.ShapeDtypeStruct(h2.shape, h2.dtype)] + cast_shapes,
        compiler_params=pltpu.CompilerParams(
            dimension_semantics=("parallel",),
            vmem_limit_bytes=VMEM_LIMIT_BYTES),
        name=f"mlp_l{layer}",
    )(*operands, *cast_sources)
    return out.reshape(bsz, seq, d), cast


def kernel(x, norm_mix_pre, norm_mix_post, norm_mlp_pre, norm_mlp_post, w_in, lru_conv_w, lru_conv_b,
           lru_w_a, lru_b_a, lru_w_x, lru_b_x, lru_lambda, pool_w, pool_scale, hgrn_lower_bound,
           hgrn_norm, sconv_w, w_branch, w_out, w_up, w_down):
    per_layer = dict(
        norm_mix_pre=norm_mix_pre, norm_mix_post=norm_mix_post, norm_mlp_pre=norm_mlp_pre,
        norm_mlp_post=norm_mlp_post, lru_conv_w=lru_conv_w, lru_conv_b=lru_conv_b,
        lru_w_a=lru_w_a, lru_b_a=lru_b_a, lru_w_x=lru_w_x, lru_b_x=lru_b_x, lru_lambda=lru_lambda,
        pool_w=pool_w, pool_scale=pool_scale, hgrn_norm=hgrn_norm, sconv_w=sconv_w)
    depth = w_in.shape[0]
    w_branch2 = w_branch.reshape(depth, N_BRANCHES * W, D_MODEL)
    mixer_sources = (w_in, w_branch2, w_out)
    mlp_sources = (w_up, w_down)
    mixer_weights = [w[0].astype(BF16) for w in mixer_sources]
    h = x
    for layer in range(depth):
        params = {name: p[layer] for name, p in per_layer.items()}
        params["hgrn_lower_bound_all"] = hgrn_lower_bound
        h, mlp_weights = _mixer_layer(h, layer, params, mixer_weights, mlp_sources)
        last = layer + 1 == depth
        h, mixer_weights = _mlp_layer(h, layer, params, mlp_weights,
                                      () if last else mixer_sources, layer + 1)
    return h
```

```python
import functools

import jax
import jax.numpy as jnp
from jax import lax
from jax.experimental import pallas as pl
from jax.experimental.pallas import tpu as pltpu

D_MODEL = 1024
N_BRANCHES = 4
W = D_MODEL // N_BRANCHES
LRU_CONV = 4
LRU_C = 8.0
POOL_WINDOWS = (2, 4, 8, 16)
POOL_GROUP = W // len(POOL_WINDOWS)
HGRN_HEADS = 4
HGRN_DK = W // HGRN_HEADS
SCONV_WIDTH = 3
D_FF = 4 * D_MODEL
N_MIX_SLOTS = 10
MIX_WIDTH = N_MIX_SLOTS * W
EPS = 1e-6

TIME_TILE = 512
HGRN_CHUNK = 64
SUBLANES = 8
EDGE_BLOCKS = 4
TAIL = 16
MLP_TILE = 1024
MLP_ROW_BLOCKS = 2
FF_CHUNK = 2048
VMEM_LIMIT_BYTES = 56 * 1024 * 1024

F32 = jnp.float32
BF16 = jnp.bfloat16


def _dot(a, b):
    return jnp.dot(a, b, preferred_element_type=F32)


def _dot_nt(a, b):
    return lax.dot_general(a, b, (((1,), (1,)), ((), ())), preferred_element_type=F32)


def _dot_tn(a, b):
    return lax.dot_general(a, b, (((0,), (0,)), ((), ())), preferred_element_type=F32)


def _rms(x, gain):
    return x * lax.rsqrt(jnp.mean(x * x, axis=-1, keepdims=True) + EPS) * gain


def _group_roll(x, s):
    n, w = x.shape
    return pltpu.roll(x.reshape(n // SUBLANES, SUBLANES, w), s, 1).reshape(n, w)


def _shift_rows(ext, s, n):
    if s == 0:
        return ext[TAIL:TAIL + n]
    return pltpu.roll(ext, s, 0)[TAIL:TAIL + n]


def _cast_slabs(refs):
    n = len(refs) // 2
    for src_ref, dst_ref in zip(refs[:n], refs[n:]):
        dst_ref[...] = src_ref[...].astype(BF16)


def _mixer_kernel(lb_layer, n_cast, h_ref, gpre_ref, gpost_ref, win_ref, convw_ref, convb_ref, wa_ref,
                  ba_ref, wx_ref, bx_ref, lam_ref, poolw_ref, pools_ref, lbraw_ref, hnorm_ref, sconvw_ref,
                  wbr_ref, wout_ref, *rest):
    cast_srcs, out_ref, cast_dsts = rest[:n_cast], rest[n_cast], rest[n_cast + 1:2 * n_cast + 1]
    tail_ref, hlru_ref, st_ref = rest[2 * n_cast + 1:]
    _cast_slabs(cast_srcs + cast_dsts)
    tt = h_ref.shape[1]
    n_chunks = tt // HGRN_CHUNK

    @pl.when(pl.program_id(1) == 0)
    def _():
        tail_ref[...] = jnp.zeros_like(tail_ref)
        hlru_ref[...] = jnp.zeros_like(hlru_ref)
        st_ref[...] = jnp.zeros_like(st_ref)

    row_blocks = [slice(i * tt // EDGE_BLOCKS, (i + 1) * tt // EDGE_BLOCKS) for i in range(EDGE_BLOCKS)]
    layer_row = lambda ref: ref[lb_layer:lb_layer + 1, :]
    u_blocks = [_rms(h_ref[0, rows, :], layer_row(gpre_ref)).astype(BF16) for rows in row_blocks]
    u = jnp.concatenate(u_blocks, axis=0)

    row = lax.broadcasted_iota(jnp.int32, (tt, W), 0)
    lane = lax.broadcasted_iota(jnp.int32, (tt, W), 1)
    sub = row & (SUBLANES - 1)
    n_groups = tt // SUBLANES

    gate_cols = D_MODEL // W
    gate_keys = lambda k: [("gate", k, j) for j in range(gate_cols)]
    pending = (gate_keys(0) + [("slot", 1)] + gate_keys(1) + gate_keys(2)
               + [("slot", s) for s in (6, 7, 8, 9)] + gate_keys(3))
    early = {}

    def issue(key):
        pending.remove(key)
        if key[0] == "slot":
            early[key] = _dot(u, win_ref[:, key[1] * W:(key[1] + 1) * W])
        else:
            lo = MIX_WIDTH + key[1] * D_MODEL + key[2] * W
            early[key] = jax.nn.sigmoid(_dot(u, win_ref[:, lo:lo + W]))

    def fill():
        if pending:
            issue(pending[0])

    def take(key):
        if key in pending:
            issue(key)
        return early.pop(key)

    def proj(slot):
        if slot == 0:
            w_slot = win_ref[:, 0:W]
            return jnp.concatenate([_dot(u_blk, w_slot) for u_blk in u_blocks], axis=0)
        if ("slot", slot) in pending or ("slot", slot) in early:
            return take(("slot", slot))
        return _dot(u, win_ref[:, slot * W:(slot + 1) * W])

    def merge(acc, k, y_k):
        gate = jnp.concatenate([take(key) for key in gate_keys(k)], axis=1)
        return acc + gate * _dot(y_k.astype(BF16), wbr_ref[k * W:(k + 1) * W, :])

    a_x = proj(0)
    ext = jnp.concatenate([tail_ref[0], a_x], axis=0)
    tail_ref[0] = a_x[tt - TAIL:]
    xc = layer_row(convb_ref) + sum(convw_ref[lb_layer, k:k + 1, :] * _shift_rows(ext, LRU_CONV - 1 - k, tt)
                              for k in range(LRU_CONV))
    fill()
    xc_b = xc.astype(BF16)
    r = jax.nn.sigmoid(_dot(xc_b, wa_ref[lb_layer]) + layer_row(ba_ref))
    i_gate = jax.nn.sigmoid(_dot(xc_b, wx_ref[lb_layer]) + layer_row(bx_ref))
    log_a = (-LRU_C) * r * jax.nn.softplus(-layer_row(lam_ref))
    a = jnp.exp(log_a)
    mult = jnp.sqrt(jnp.tanh(-log_a) * (1.0 + a * a))
    uu = mult * i_gate * xc
    fill()
    s = 1
    while s < SUBLANES:
        keep = sub >= s
        a_sh = jnp.where(keep, _group_roll(a, s), 1.0)
        u_sh = jnp.where(keep, _group_roll(uu, s), 0.0)
        uu = uu + a * u_sh
        a = a * a_sh
        s *= 2
    fill()
    a_g = a.reshape(n_groups, SUBLANES, W)
    u_g = uu.reshape(n_groups, SUBLANES, W)
    h_prev = hlru_ref[0:1, :]
    h_groups = []
    for g in range(n_groups):
        h_g = u_g[g] + a_g[g] * h_prev
        h_groups.append(h_g)
        h_prev = h_g[SUBLANES - 1:SUBLANES, :]
    hlru_ref[0:1, :] = h_prev
    h_seq = jnp.concatenate(h_groups, axis=0)
    y_a = h_seq * jax.nn.gelu(proj(1))
    merged = merge(jnp.zeros((tt, D_MODEL), F32), 0, y_a)

    p_x = proj(2)
    ext = jnp.concatenate([tail_ref[1], p_x], axis=0)
    tail_ref[1] = p_x[tt - TAIL:]
    group = lax.shift_right_logical(lane, POOL_GROUP.bit_length() - 1)
    win_sum = ext
    mean = jnp.zeros((tt, W), F32)
    pos = (pl.program_id(1) * tt + row + 1).astype(F32)
    for g, win in enumerate(POOL_WINDOWS):
        win_sum = win_sum + pltpu.roll(win_sum, win // 2, 0)
        mean = jnp.where(group == g, win_sum[TAIL:] / jnp.minimum(pos, float(win)), mean)
    fill()
    y_b = _dot((mean - p_x).astype(BF16), poolw_ref[lb_layer]) * layer_row(pools_ref)
    merged = merge(merged, 1, y_b)

    c_q = proj(3)
    c_f = proj(4)
    v = proj(5)
    lb_sm = jax.nn.softmax(lbraw_ref[...], axis=0)
    lb = jnp.sum(lb_sm[1:lb_layer + 1], axis=0, keepdims=True) if lb_layer else jnp.zeros((1, W), F32)
    f = lb + (1.0 - lb) * jax.nn.sigmoid(c_f)
    lf = jnp.log(f)
    kk = 1.0 - f
    b = lf
    s = 1
    while s < SUBLANES:
        b = b + jnp.where(sub >= s, _group_roll(b, s), 0.0)
        s *= 2
    b_g = b.reshape(n_groups, SUBLANES, W)
    b_parts = []
    for g in range(n_groups):
        if g % (HGRN_CHUNK // SUBLANES) == 0:
            b_parts.append(b_g[g])
        else:
            b_parts.append(b_g[g] + b_parts[-1][SUBLANES - 1:SUBLANES, :])
    b = jnp.concatenate(b_parts, axis=0)
    fill()
    b3 = b.reshape(n_chunks, HGRN_CHUNK, W)
    b_last = b3[:, HGRN_CHUNK - 1:HGRN_CHUNK, :]
    qe_b = (c_q * jnp.exp(b)).astype(BF16)
    kd_b = (kk.reshape(n_chunks, HGRN_CHUNK, W) * jnp.exp(b_last - b3)).reshape(tt, W).astype(BF16)
    decay_chunk = jnp.exp(b_last)
    v_b = v.astype(BF16)

    sq_r = lax.broadcasted_iota(jnp.int32, (W, W), 0)
    sq_c = lax.broadcasted_iota(jnp.int32, (W, W), 1)
    same_head = (sq_r ^ sq_c) < HGRN_DK
    head_ones = jnp.where(same_head, 1.0, 0.0).astype(BF16)

    pair_xor = (lax.broadcasted_iota(jnp.int32, (HGRN_CHUNK, W), 0)
                ^ (lax.broadcasted_iota(jnp.int32, (HGRN_CHUNK, W), 1) & (HGRN_CHUNK - 1)))
    levels = []
    half = SUBLANES
    while half < HGRN_CHUNK:
        blk = 2 * half
        upper = (row & half) != 0
        bb = b.reshape(tt // blk, blk, W)
        delta = (bb - bb[:, half - 1:half, :]).reshape(tt, W)
        q_l = jnp.where(upper, c_q * jnp.exp(jnp.minimum(delta, 0.0)), 0.0)
        k_l = jnp.where(upper, 0.0, kk * jnp.exp(jnp.minimum(-delta, 0.0)))
        levels.append((q_l.astype(BF16), k_l.astype(BF16), blk))
        half = blk

    chunk_rows = [slice(c * HGRN_CHUNK, (c + 1) * HGRN_CHUNK) for c in range(n_chunks)]

    def head_stack(x_chunk):
        return jnp.where(same_head, jnp.concatenate([x_chunk] * HGRN_HEADS, axis=0), 0.0)

    updates = []
    for c, sl in enumerate(chunk_rows):
        updates.append(_dot_tn(v_b[sl], kd_b[sl]))
        if c % 2 == 1:
            fill()
    probs = []
    n_small = 0
    for c, sl in enumerate(chunk_rows):
        scores = None
        for q_l, k_l, xor_bound in reversed(levels):
            level_scores = _dot_nt(q_l[sl], head_stack(k_l[sl]))
            scores = level_scores if scores is None else jnp.where(pair_xor < xor_bound, level_scores, scores)
            n_small += 1
            if n_small % 2 == 0:
                fill()
        probs.append(jnp.where(pair_xor < SUBLANES, 0.0, scores).astype(BF16))

    assert n_chunks == SUBLANES
    st = st_ref[...]
    o_parts = []
    o_same = None
    decay = None
    f_d = f
    for d, sl in enumerate(chunk_rows):
        o_c = _dot_nt(qe_b[sl], st.astype(BF16))
        o_parts.append(o_c + _dot(probs[d], head_stack(v_b[sl])))
        st = st * decay_chunk[d] + jnp.where(same_head, updates[d], 0.0)
        if d == 0:
            w_d, v_d = c_q * kk, v
        else:
            decay = f if d == 1 else decay * f_d
            f_d = _group_roll(f, d)
            w_d = jnp.where(sub >= d, c_q * (1.0 - f_d) * decay, 0.0)
            v_d = _group_roll(v, d)
        term = _dot(w_d.astype(BF16), head_ones) * v_d
        o_same = term if o_same is None else o_same + term
    st_ref[...] = st
    o = jnp.concatenate(o_parts, axis=0) + o_same

    ms = _dot((o * o).astype(BF16), head_ones) * (1.0 / HGRN_DK)
    y_c = o * lax.rsqrt(ms + EPS) * layer_row(hnorm_ref) * jax.nn.silu(proj(6))
    merged = merge(merged, 2, y_c)

    s_b = proj(7)
    z = proj(8) * proj(9)
    ext = jnp.concatenate([tail_ref[2], z], axis=0)
    tail_ref[2] = z[tt - TAIL:]
    y_d = s_b * sum(sconvw_ref[lb_layer, k:k + 1, :] * _shift_rows(ext, SCONV_WIDTH - 1 - k, tt)
                    for k in range(SCONV_WIDTH))

    merged = merge(merged, 3, y_d)
    merged_b = merged.astype(BF16)
    for rows in row_blocks:
        mix = _dot(merged_b[rows], wout_ref[...])
        out_ref[0, rows, :] = h_ref[0, rows, :] + _rms(mix, layer_row(gpost_ref))


def _mlp_kernel(layer, n_cast, h_ref, gpre_ref, gpost_ref, wup_ref, wdown_ref, *rest):
    cast_srcs, out_ref, cast_dsts = rest[:n_cast], rest[n_cast], rest[n_cast + 1:]
    _cast_slabs(cast_srcs + cast_dsts)
    gain_pre, gain_post = gpre_ref[layer:layer + 1, :], gpost_ref[layer:layer + 1, :]
    n_rows = h_ref.shape[0]
    for i in range(MLP_ROW_BLOCKS):
        rows = slice(i * n_rows // MLP_ROW_BLOCKS, (i + 1) * n_rows // MLP_ROW_BLOCKS)
        h_in = h_ref[rows, :]
        u = _rms(h_in, gain_pre).astype(BF16)
        m = jnp.zeros(h_in.shape, F32)
        for j in range(D_FF // FF_CHUNK):
            cols = slice(j * FF_CHUNK, (j + 1) * FF_CHUNK)
            hid = jnp.square(jnp.maximum(_dot(u, wup_ref[:, cols]), 0.0))
            m = m + _dot(hid.astype(BF16), wdown_ref[cols, :])
        out_ref[rows, :] = h_in + _rms(m, gain_post)


def _whole(arr):
    nd = arr.ndim
    return pl.BlockSpec(arr.shape, lambda *_: (0,) * nd, pipeline_mode=pl.Buffered(1))


def _cast_specs(sources, layer, n_steps, step_of):
    in_specs, out_specs, out_shapes = [], [], []
    for src in sources:
        _, rows, cols = src.shape
        slab = rows // n_steps
        in_specs.append(pl.BlockSpec((None, slab, cols), lambda *g: (layer, step_of(*g), 0)))
        out_specs.append(pl.BlockSpec((slab, cols), lambda *g: (step_of(*g), 0)))
        out_shapes.append(jax.ShapeDtypeStruct((rows, cols), BF16))
    return in_specs, out_specs, out_shapes


def _block_diag(blocks):
    depth, n, bi, bo = blocks.shape
    eye = jnp.eye(n, dtype=blocks.dtype)
    return jnp.einsum("lnio,nm->lnimo", blocks, eye).reshape(depth, n * bi, n * bo).astype(BF16)


def _mixer_layer(h, layer, params, weights, cast_sources):
    bsz, seq, _ = h.shape
    n_t = seq // TIME_TILE
    w_in_b, w_branch_b, w_out_b = weights
    operands = [
        h,
        params["norm_mix_pre"], params["norm_mix_post"],
        w_in_b,
        params["lru_conv_w"], params["lru_conv_b"],
        params["lru_w_a_dense"], params["lru_b_a"],
        params["lru_w_x_dense"], params["lru_b_x"],
        params["lru_lambda"],
        params["pool_w_dense"], params["pool_scale"],
        params["hgrn_lower_bound"],
        params["hgrn_norm_heads"],
        params["sconv_w"],
        w_branch_b,
        w_out_b,
    ]
    cast_in, cast_out, cast_shapes = _cast_specs(cast_sources, layer, bsz * n_t, lambda bi, ti: bi * n_t + ti)
    in_specs = [pl.BlockSpec((1, TIME_TILE, D_MODEL), lambda bi, ti: (bi, ti, 0))]
    in_specs += [_whole(op) for op in operands[1:]] + cast_in
    h_out, *cast = pl.pallas_call(
        functools.partial(_mixer_kernel, layer, len(cast_sources)),
        grid=(bsz, n_t),
        in_specs=in_specs,
        out_specs=[pl.BlockSpec((1, TIME_TILE, D_MODEL), lambda bi, ti: (bi, ti, 0))] + cast_out,
        out_shape=[jax.ShapeDtypeStruct(h.shape, h.dtype)] + cast_shapes,
        scratch_shapes=[
            pltpu.VMEM((3, TAIL, W), F32),
            pltpu.VMEM((8, W), F32),
            pltpu.VMEM((W, W), F32),
        ],
        compiler_params=pltpu.CompilerParams(
            dimension_semantics=("parallel", "arbitrary"),
            vmem_limit_bytes=VMEM_LIMIT_BYTES),
        name=f"mixer_l{layer}",
    )(*operands, *cast_sources)
    return h_out, cast


def _mlp_layer(h, layer, params, weights, cast_sources, cast_layer):
    bsz, seq, d = h.shape
    h2 = h.reshape(bsz * seq, d)
    n_steps = bsz * seq // MLP_TILE
    operands = [h2, params["norm_mlp_pre"], params["norm_mlp_post"], *weights]
    cast_in, cast_out, cast_shapes = _cast_specs(cast_sources, cast_layer, n_steps, lambda i: i)
    in_specs = [pl.BlockSpec((MLP_TILE, d), lambda i: (i, 0))] + [_whole(op) for op in operands[1:]] + cast_in
    out, *cast = pl.pallas_call(
        functools.partial(_mlp_kernel, layer, len(cast_sources)),
        grid=(n_steps,),
        in_specs=in_specs,
        out_specs=[pl.BlockSpec((MLP_TILE, d), lambda i: (i, 0))] + cast_out,
        out_shape=[jax.ShapeDtypeStruct(h2.shape, h2.dtype)] + cast_shapes,
        compiler_params=pltpu.CompilerParams(
            dimension_semantics=("parallel",),
            vmem_limit_bytes=VMEM_LIMIT_BYTES),
        name=f"mlp_l{layer}",
    )(*operands, *cast_sources)
    return out.reshape(bsz, seq, d), cast


def kernel(x, norm_mix_pre, norm_mix_post, norm_mlp_pre, norm_mlp_post, w_in, lru_conv_w, lru_conv_b,
           lru_w_a, lru_b_a, lru_w_x, lru_b_x, lru_lambda, pool_w, pool_scale, hgrn_lower_bound,
           hgrn_norm, sconv_w, w_branch, w_out, w_up, w_down):
    params = dict(
        norm_mix_pre=norm_mix_pre, norm_mix_post=norm_mix_post, norm_mlp_pre=norm_mlp_pre,
        norm_mlp_post=norm_mlp_post, lru_conv_w=lru_conv_w, lru_conv_b=lru_conv_b,
        lru_w_a_dense=_block_diag(lru_w_a), lru_b_a=lru_b_a, lru_w_x_dense=_block_diag(lru_w_x),
        lru_b_x=lru_b_x, lru_lambda=lru_lambda, pool_w_dense=_block_diag(pool_w), pool_scale=pool_scale,
        hgrn_lower_bound=hgrn_lower_bound, hgrn_norm_heads=jnp.tile(hgrn_norm, (1, HGRN_HEADS)),
        sconv_w=sconv_w)
    depth = w_in.shape[0]
    w_branch2 = w_branch.reshape(depth, N_BRANCHES * W, D_MODEL)
    mixer_sources = (w_in, w_branch2, w_out)
    mlp_sources = (w_up, w_down)
    mixer_weights = [w[0].astype(BF16) for w in mixer_sources]
    h = x
    for layer in range(depth):
        h, mlp_weights = _mixer_layer(h, layer, params, mixer_weights, mlp_sources)
        last = layer + 1 == depth
        h, mixer_weights = _mlp_layer(h, layer, params, mlp_weights,
                                      () if last else mixer_sources, layer + 1)
    return h
```

```python
import functools

import jax
import jax.numpy as jnp
from jax import lax
from jax.experimental import pallas as pl
from jax.experimental.pallas import tpu as pltpu

D_MODEL = 1024
N_BRANCHES = 4
W = D_MODEL // N_BRANCHES
LRU_CONV = 4
LRU_C = 8.0
POOL_WINDOWS = (2, 4, 8, 16)
POOL_GROUP = W // len(POOL_WINDOWS)
HGRN_HEADS = 4
HGRN_DK = W // HGRN_HEADS
SCONV_WIDTH = 3
D_FF = 4 * D_MODEL
N_MIX_SLOTS = 10
MIX_WIDTH = N_MIX_SLOTS * W
EPS = 1e-6

TIME_TILE = 512
HGRN_CHUNK = 64
SUBLANES = 8
EDGE_BLOCKS = 4
TAIL = 16
MLP_TILE = 1024
MLP_ROW_BLOCKS = 2
FF_CHUNK = 2048
VMEM_LIMIT_BYTES = 56 * 1024 * 1024

F32 = jnp.float32
BF16 = jnp.bfloat16


def _dot(a, b):
    return jnp.dot(a, b, preferred_element_type=F32)


def _dot_nt(a, b):
    return lax.dot_general(a, b, (((1,), (1,)), ((), ())), preferred_element_type=F32)


def _dot_tn(a, b):
    return lax.dot_general(a, b, (((0,), (0,)), ((), ())), preferred_element_type=F32)


def _rms(x, gain):
    return x * lax.rsqrt(jnp.mean(x * x, axis=-1, keepdims=True) + EPS) * gain


def _group_roll(x, s):
    n, w = x.shape
    return pltpu.roll(x.reshape(n // SUBLANES, SUBLANES, w), s, 1).reshape(n, w)


def _shift_rows(ext, s, n):
    if s == 0:
        return ext[TAIL:TAIL + n]
    return pltpu.roll(ext, s, 0)[TAIL:TAIL + n]


def _cast_slabs(refs):
    n = len(refs) // 2
    for src_ref, dst_ref in zip(refs[:n], refs[n:]):
        dst_ref[...] = src_ref[...].astype(BF16)


def _mixer_kernel(lb_layer, n_cast, h_ref, gpre_ref, gpost_ref, win_ref, convw_ref, convb_ref, wa_ref,
                  ba_ref, wx_ref, bx_ref, lam_ref, poolw_ref, pools_ref, lbraw_ref, hnorm_ref, sconvw_ref,
                  wbr_ref, wout_ref, *rest):
    cast_srcs, out_ref, cast_dsts = rest[:n_cast], rest[n_cast], rest[n_cast + 1:2 * n_cast + 1]
    tail_ref, hlru_ref, st_ref = rest[2 * n_cast + 1:]
    _cast_slabs(cast_srcs + cast_dsts)
    tt = h_ref.shape[1]
    n_chunks = tt // HGRN_CHUNK

    @pl.when(pl.program_id(1) == 0)
    def _():
        tail_ref[...] = jnp.zeros_like(tail_ref)
        hlru_ref[...] = jnp.zeros_like(hlru_ref)
        st_ref[...] = jnp.zeros_like(st_ref)

    row_blocks = [slice(i * tt // EDGE_BLOCKS, (i + 1) * tt // EDGE_BLOCKS) for i in range(EDGE_BLOCKS)]
    layer_row = lambda ref: ref[lb_layer:lb_layer + 1, :]
    u_blocks = [_rms(h_ref[0, rows, :], layer_row(gpre_ref)).astype(BF16) for rows in row_blocks]
    u = jnp.concatenate(u_blocks, axis=0)

    row = lax.broadcasted_iota(jnp.int32, (tt, W), 0)
    lane = lax.broadcasted_iota(jnp.int32, (tt, W), 1)
    sub = row & (SUBLANES - 1)
    n_groups = tt // SUBLANES

    gate_cols = D_MODEL // W
    gate_keys = lambda k: [("gate", k, j) for j in range(gate_cols)]
    pending = (gate_keys(0) + [("slot", 1)] + gate_keys(1) + gate_keys(2)
               + [("slot", s) for s in (6, 7, 8, 9)] + gate_keys(3))
    early = {}

    def issue(key):
        pending.remove(key)
        if key[0] == "slot":
            early[key] = _dot(u, win_ref[:, key[1] * W:(key[1] + 1) * W])
        else:
            lo = MIX_WIDTH + key[1] * D_MODEL + key[2] * W
            early[key] = jax.nn.sigmoid(_dot(u, win_ref[:, lo:lo + W]))

    def fill():
        if pending:
            issue(pending[0])

    def take(key):
        if key in pending:
            issue(key)
        return early.pop(key)

    def proj(slot):
        if slot == 0:
            w_slot = win_ref[:, 0:W]
            return jnp.concatenate([_dot(u_blk, w_slot) for u_blk in u_blocks], axis=0)
        if ("slot", slot) in pending or ("slot", slot) in early:
            return take(("slot", slot))
        return _dot(u, win_ref[:, slot * W:(slot + 1) * W])

    def merge(acc, k, y_k):
        gate = jnp.concatenate([take(key) for key in gate_keys(k)], axis=1)
        return acc + gate * _dot(y_k.astype(BF16), wbr_ref[k * W:(k + 1) * W, :])

    a_x = proj(0)
    ext = jnp.concatenate([tail_ref[0], a_x], axis=0)
    tail_ref[0] = a_x[tt - TAIL:]
    xc = layer_row(convb_ref) + sum(convw_ref[lb_layer, k:k + 1, :] * _shift_rows(ext, LRU_CONV - 1 - k, tt)
                              for k in range(LRU_CONV))
    fill()
    xc_b = xc.astype(BF16)
    r = jax.nn.sigmoid(_dot(xc_b, wa_ref[lb_layer]) + layer_row(ba_ref))
    i_gate = jax.nn.sigmoid(_dot(xc_b, wx_ref[lb_layer]) + layer_row(bx_ref))
    log_a = (-LRU_C) * r * jax.nn.softplus(-layer_row(lam_ref))
    a = jnp.exp(log_a)
    mult = jnp.sqrt(jnp.tanh(-log_a) * (1.0 + a * a))
    uu = mult * i_gate * xc
    fill()
    s = 1
    while s < SUBLANES:
        keep = sub >= s
        a_sh = jnp.where(keep, _group_roll(a, s), 1.0)
        u_sh = jnp.where(keep, _group_roll(uu, s), 0.0)
        uu = uu + a * u_sh
        a = a * a_sh
        s *= 2
    fill()
    a_g = a.reshape(n_groups, SUBLANES, W)
    u_g = uu.reshape(n_groups, SUBLANES, W)
    h_prev = hlru_ref[0:1, :]
    h_groups = []
    for g in range(n_groups):
        h_g = u_g[g] + a_g[g] * h_prev
        h_groups.append(h_g)
        h_prev = h_g[SUBLANES - 1:SUBLANES, :]
    hlru_ref[0:1, :] = h_prev
    h_seq = jnp.concatenate(h_groups, axis=0)
    y_a = h_seq * jax.nn.gelu(proj(1))
    merged = merge(jnp.zeros((tt, D_MODEL), F32), 0, y_a)

    p_x = proj(2)
    ext = jnp.concatenate([tail_ref[1], p_x], axis=0)
    tail_ref[1] = p_x[tt - TAIL:]
    group = lax.shift_right_logical(lane, POOL_GROUP.bit_length() - 1)
    win_sum = ext
    mean = jnp.zeros((tt, W), F32)
    pos = (pl.program_id(1) * tt + row + 1).astype(F32)
    for g, win in enumerate(POOL_WINDOWS):
        win_sum = win_sum + pltpu.roll(win_sum, win // 2, 0)
        mean = jnp.where(group == g, win_sum[TAIL:] / jnp.minimum(pos, float(win)), mean)
    fill()
    y_b = _dot((mean - p_x).astype(BF16), poolw_ref[lb_layer]) * layer_row(pools_ref)
    merged = merge(merged, 1, y_b)

    c_q = proj(3)
    c_f = proj(4)
    v = proj(5)
    lb_sm = jax.nn.softmax(lbraw_ref[...], axis=0)
    lb = jnp.sum(lb_sm[1:lb_layer + 1], axis=0, keepdims=True) if lb_layer else jnp.zeros((1, W), F32)
    f = lb + (1.0 - lb) * jax.nn.sigmoid(c_f)
    lf = jnp.log(f)
    kk = 1.0 - f
    b = lf
    s = 1
    while s < SUBLANES:
        b = b + jnp.where(sub >= s, _group_roll(b, s), 0.0)
        s *= 2
    b_g = b.reshape(n_groups, SUBLANES, W)
    b_parts = []
    for g in range(n_groups):
        if g % (HGRN_CHUNK // SUBLANES) == 0:
            b_parts.append(b_g[g])
        else:
            b_parts.append(b_g[g] + b_parts[-1][SUBLANES - 1:SUBLANES, :])
    b = jnp.concatenate(b_parts, axis=0)
    fill()
    b3 = b.reshape(n_chunks, HGRN_CHUNK, W)
    b_last = b3[:, HGRN_CHUNK - 1:HGRN_CHUNK, :]
    qe_b = (c_q * jnp.exp(b)).astype(BF16)
    kd_b = (kk.reshape(n_chunks, HGRN_CHUNK, W) * jnp.exp(b_last - b3)).reshape(tt, W).astype(BF16)
    decay_chunk = jnp.exp(b_last)
    v_b = v.astype(BF16)

    sq_r = lax.broadcasted_iota(jnp.int32, (W, W), 0)
    sq_c = lax.broadcasted_iota(jnp.int32, (W, W), 1)
    same_head = (sq_r ^ sq_c) < HGRN_DK
    head_ones = jnp.where(same_head, 1.0, 0.0).astype(BF16)

    half_rows = HGRN_CHUNK // 2
    groups_per_chunk = HGRN_CHUNK // SUBLANES
    levels = []
    half = SUBLANES
    while half < HGRN_CHUNK:
        blk = 2 * half
        bb = b.reshape(tt // blk, blk, W)
        delta = (bb - bb[:, half - 1:half, :]).reshape(tt, W)
        levels.append((half, c_q * jnp.exp(jnp.minimum(delta, 0.0)), kk * jnp.exp(jnp.minimum(-delta, 0.0))))
        half = blk

    def half_groups(c, half, second):
        return [c * groups_per_chunk + g for g in range(groups_per_chunk)
                if ((g * SUBLANES) & half != 0) == second]

    def gather(x, groups):
        return jnp.concatenate([x[g * SUBLANES:(g + 1) * SUBLANES] for g in groups], axis=0).astype(BF16)

    stack_r = lax.broadcasted_iota(jnp.int32, (HGRN_HEADS * half_rows, W), 0)
    stack_c = lax.broadcasted_iota(jnp.int32, (HGRN_HEADS * half_rows, W), 1)
    stack_same_head = (stack_r // half_rows) == (stack_c // HGRN_DK)

    def head_stack(x_rows):
        return jnp.where(stack_same_head, jnp.concatenate([x_rows] * HGRN_HEADS, axis=0), 0.0)

    pair_t = lax.broadcasted_iota(jnp.int32, (half_rows, HGRN_HEADS * half_rows), 0)
    pair_s = lax.broadcasted_iota(jnp.int32, (half_rows, HGRN_HEADS * half_rows), 1) % half_rows

    chunk_rows = [slice(c * HGRN_CHUNK, (c + 1) * HGRN_CHUNK) for c in range(n_chunks)]
    updates = []
    for c, sl in enumerate(chunk_rows):
        updates.append(_dot_tn(v_b[sl], kd_b[sl]))
        if c % 2 == 1:
            fill()
    n_small = 0
    probs = {}
    for c in range(n_chunks):
        for half, q_fac, k_fac in levels:
            scores = _dot_nt(gather(q_fac, half_groups(c, half, True)),
                             head_stack(gather(k_fac, half_groups(c, half, False))))
            if 2 * half < HGRN_CHUNK:
                scores = jnp.where((pair_t // half) == (pair_s // half), scores, 0.0)
            probs[c, half] = scores.astype(BF16)
            n_small += 1
            if n_small % 2 == 0:
                fill()
    intra = [[None] * groups_per_chunk for _ in range(n_chunks)]
    for c in range(n_chunks):
        for half, _, _ in levels:
            o_l = _dot(probs[c, half], head_stack(gather(v, half_groups(c, half, False))))
            for i, g in enumerate(half_groups(c, half, True)):
                part = o_l[i * SUBLANES:(i + 1) * SUBLANES]
                g_in = g - c * groups_per_chunk
                intra[c][g_in] = part if intra[c][g_in] is None else intra[c][g_in] + part
            n_small += 1
            if n_small % 2 == 0:
                fill()

    assert n_chunks == SUBLANES
    st = st_ref[...]
    o_parts = []
    o_same = None
    decay = None
    f_d = f
    for d, sl in enumerate(chunk_rows):
        o_c = _dot_nt(qe_b[sl], st.astype(BF16))
        o_parts.append(o_c + jnp.concatenate(
            [jnp.zeros((SUBLANES, W), F32) if part is None else part for part in intra[d]], axis=0))
        st = st * decay_chunk[d] + jnp.where(same_head, updates[d], 0.0)
        if d == 0:
            w_d, v_d = c_q * kk, v
        else:
            decay = f if d == 1 else decay * f_d
            f_d = _group_roll(f, d)
            w_d = jnp.where(sub >= d, c_q * (1.0 - f_d) * decay, 0.0)
            v_d = _group_roll(v, d)
        term = _dot(w_d.astype(BF16), head_ones) * v_d
        o_same = term if o_same is None else o_same + term
    st_ref[...] = st
    o = jnp.concatenate(o_parts, axis=0) + o_same

    ms = _dot((o * o).astype(BF16), head_ones) * (1.0 / HGRN_DK)
    y_c = o * lax.rsqrt(ms + EPS) * layer_row(hnorm_ref) * jax.nn.silu(proj(6))
    merged = merge(merged, 2, y_c)

    s_b = proj(7)
    z = proj(8) * proj(9)
    ext = jnp.concatenate([tail_ref[2], z], axis=0)
    tail_ref[2] = z[tt - TAIL:]
    y_d = s_b * sum(sconvw_ref[lb_layer, k:k + 1, :] * _shift_rows(ext, SCONV_WIDTH - 1 - k, tt)
                    for k in range(SCONV_WIDTH))

    merged = merge(merged, 3, y_d)
    merged_b = merged.astype(BF16)
    for rows in row_blocks:
        mix = _dot(merged_b[rows], wout_ref[...])
        out_ref[0, rows, :] = h_ref[0, rows, :] + _rms(mix, layer_row(gpost_ref))


def _mlp_kernel(layer, n_cast, h_ref, gpre_ref, gpost_ref, wup_ref, wdown_ref, *rest):
    cast_srcs, out_ref, cast_dsts = rest[:n_cast], rest[n_cast], rest[n_cast + 1:]
    _cast_slabs(cast_srcs + cast_dsts)
    gain_pre, gain_post = gpre_ref[layer:layer + 1, :], gpost_ref[layer:layer + 1, :]
    n_rows = h_ref.shape[0]
    for i in range(MLP_ROW_BLOCKS):
        rows = slice(i * n_rows // MLP_ROW_BLOCKS, (i + 1) * n_rows // MLP_ROW_BLOCKS)
        h_in = h_ref[rows, :]
        u = _rms(h_in, gain_pre).astype(BF16)
        m = jnp.zeros(h_in.shape, F32)
        for j in range(D_FF // FF_CHUNK):
            cols = slice(j * FF_CHUNK, (j + 1) * FF_CHUNK)
            hid = jnp.square(jnp.maximum(_dot(u, wup_ref[:, cols]), 0.0))
            m = m + _dot(hid.astype(BF16), wdown_ref[cols, :])
        out_ref[rows, :] = h_in + _rms(m, gain_post)


def _whole(arr):
    nd = arr.ndim
    return pl.BlockSpec(arr.shape, lambda *_: (0,) * nd, pipeline_mode=pl.Buffered(1))


def _cast_specs(sources, layer, n_steps, step_of):
    in_specs, out_specs, out_shapes = [], [], []
    for src in sources:
        _, rows, cols = src.shape
        slab = rows // n_steps
        in_specs.append(pl.BlockSpec((None, slab, cols), lambda *g: (layer, step_of(*g), 0)))
        out_specs.append(pl.BlockSpec((slab, cols), lambda *g: (step_of(*g), 0)))
        out_shapes.append(jax.ShapeDtypeStruct((rows, cols), BF16))
    return in_specs, out_specs, out_shapes


def _block_diag(blocks):
    depth, n, bi, bo = blocks.shape
    eye = jnp.eye(n, dtype=blocks.dtype)
    return jnp.einsum("lnio,nm->lnimo", blocks, eye).reshape(depth, n * bi, n * bo).astype(BF16)


def _mixer_layer(h, layer, params, weights, cast_sources):
    bsz, seq, _ = h.shape
    n_t = seq // TIME_TILE
    w_in_b, w_branch_b, w_out_b = weights
    operands = [
        h,
        params["norm_mix_pre"], params["norm_mix_post"],
        w_in_b,
        params["lru_conv_w"], params["lru_conv_b"],
        params["lru_w_a_dense"], params["lru_b_a"],
        params["lru_w_x_dense"], params["lru_b_x"],
        params["lru_lambda"],
        params["pool_w_dense"], params["pool_scale"],
        params["hgrn_lower_bound"],
        params["hgrn_norm_heads"],
        params["sconv_w"],
        w_branch_b,
        w_out_b,
    ]
    cast_in, cast_out, cast_shapes = _cast_specs(cast_sources, layer, bsz * n_t, lambda bi, ti: bi * n_t + ti)
    in_specs = [pl.BlockSpec((1, TIME_TILE, D_MODEL), lambda bi, ti: (bi, ti, 0))]
    in_specs += [_whole(op) for op in operands[1:]] + cast_in
    h_out, *cast = pl.pallas_call(
        functools.partial(_mixer_kernel, layer, len(cast_sources)),
        grid=(bsz, n_t),
        in_specs=in_specs,
        out_specs=[pl.BlockSpec((1, TIME_TILE, D_MODEL), lambda bi, ti: (bi, ti, 0))] + cast_out,
        out_shape=[jax.ShapeDtypeStruct(h.shape, h.dtype)] + cast_shapes,
        scratch_shapes=[
            pltpu.VMEM((3, TAIL, W), F32),
            pltpu.VMEM((8, W), F32),
            pltpu.VMEM((W, W), F32),
        ],
        compiler_params=pltpu.CompilerParams(
            dimension_semantics=("parallel", "arbitrary"),
            vmem_limit_bytes=VMEM_LIMIT_BYTES),
        name=f"mixer_l{layer}",
    )(*operands, *cast_sources)
    return h_out, cast


def _mlp_layer(h, layer, params, weights, cast_sources, cast_layer):
    bsz, seq, d = h.shape
    h2 = h.reshape(bsz * seq, d)
    n_steps = bsz * seq // MLP_TILE
    operands = [h2, params["norm_mlp_pre"], params["norm_mlp_post"], *weights]
    cast_in, cast_out, cast_shapes = _cast_specs(cast_sources, cast_layer, n_steps, lambda i: i)
    in_specs = [pl.BlockSpec((MLP_TILE, d), lambda i: (i, 0))] + [_whole(op) for op in operands[1:]] + cast_in
    out, *cast = pl.pallas_call(
        functools.partial(_mlp_kernel, layer, len(cast_sources)),
        grid=(n_steps,),
        in_specs=in_specs,
        out_specs=[pl.BlockSpec((MLP_TILE, d), lambda i: (i, 0))] + cast_out,
        out_shape=[jax.ShapeDtypeStruct(h2.shape, h2.dtype)] + cast_shapes,
        compiler_params=pltpu.CompilerParams(
            dimension_semantics=("parallel",),
            vmem_limit_bytes=VMEM_LIMIT_BYTES),
        name=f"mlp_l{layer}",
    )(*operands, *cast_sources)
    return out.reshape(bsz, seq, d), cast


def kernel(x, norm_mix_pre, norm_mix_post, norm_mlp_pre, norm_mlp_post, w_in, lru_conv_w, lru_conv_b,
           lru_w_a, lru_b_a, lru_w_x, lru_b_x, lru_lambda, pool_w, pool_scale, hgrn_lower_bound,
           hgrn_norm, sconv_w, w_branch, w_out, w_up, w_down):
    params = dict(
        norm_mix_pre=norm_mix_pre, norm_mix_post=norm_mix_post, norm_mlp_pre=norm_mlp_pre,
        norm_mlp_post=norm_mlp_post, lru_conv_w=lru_conv_w, lru_conv_b=lru_conv_b,
        lru_w_a_dense=_block_diag(lru_w_a), lru_b_a=lru_b_a, lru_w_x_dense=_block_diag(lru_w_x),
        lru_b_x=lru_b_x, lru_lambda=lru_lambda, pool_w_dense=_block_diag(pool_w), pool_scale=pool_scale,
        hgrn_lower_bound=hgrn_lower_bound, hgrn_norm_heads=jnp.tile(hgrn_norm, (1, HGRN_HEADS)),
        sconv_w=sconv_w)
    depth = w_in.shape[0]
    w_branch2 = w_branch.reshape(depth, N_BRANCHES * W, D_MODEL)
    mixer_sources = (w_in, w_branch2, w_out)
    mlp_sources = (w_up, w_down)
    mixer_weights = [w[0].astype(BF16) for w in mixer_sources]
    h = x
    for layer in range(depth):
        h, mlp_weights = _mixer_layer(h, layer, params, mixer_weights, mlp_sources)
        last = layer + 1 == depth
        h, mixer_weights = _mlp_layer(h, layer, params, mlp_weights,
                                      () if last else mixer_sources, layer + 1)
    return h
```

```python
import functools

import jax
import jax.numpy as jnp
from jax import lax
from jax.experimental import pallas as pl
from jax.experimental.pallas import tpu as pltpu

D_MODEL = 1024
N_BRANCHES = 4
W = D_MODEL // N_BRANCHES
LRU_CONV = 4
LRU_C = 8.0
POOL_WINDOWS = (2, 4, 8, 16)
POOL_GROUP = W // len(POOL_WINDOWS)
HGRN_HEADS = 4
HGRN_DK = W // HGRN_HEADS
SCONV_WIDTH = 3
D_FF = 4 * D_MODEL
N_MIX_SLOTS = 10
MIX_WIDTH = N_MIX_SLOTS * W
EPS = 1e-6

TIME_TILE = 512
HGRN_CHUNK = 64
SUBLANES = 8
EDGE_BLOCKS = 4
TAIL = 16
MLP_TILE = 1024
MLP_ROW_BLOCKS = 2
FF_CHUNK = 2048
VMEM_LIMIT_BYTES = 56 * 1024 * 1024

F32 = jnp.float32
BF16 = jnp.bfloat16


def _dot(a, b):
    return jnp.dot(a, b, preferred_element_type=F32)


def _dot_nt(a, b):
    return lax.dot_general(a, b, (((1,), (1,)), ((), ())), preferred_element_type=F32)


def _dot_tn(a, b):
    return lax.dot_general(a, b, (((0,), (0,)), ((), ())), preferred_element_type=F32)


def _rms(x, gain):
    return x * lax.rsqrt(jnp.mean(x * x, axis=-1, keepdims=True) + EPS) * gain


def _group_roll(x, s):
    n, w = x.shape
    return pltpu.roll(x.reshape(n // SUBLANES, SUBLANES, w), s, 1).reshape(n, w)


def _shift_rows(ext, s, n):
    if s == 0:
        return ext[TAIL:TAIL + n]
    return pltpu.roll(ext, s, 0)[TAIL:TAIL + n]


def _cast_slabs(refs):
    n = len(refs) // 2
    for src_ref, dst_ref in zip(refs[:n], refs[n:]):
        dst_ref[...] = src_ref[...].astype(BF16)


def _mixer_kernel(lb_layer, n_cast, h_ref, gpre_ref, gpost_ref, win_ref, convw_ref, convb_ref, wa_ref,
                  ba_ref, wx_ref, bx_ref, lam_ref, poolw_ref, pools_ref, lbraw_ref, hnorm_ref, sconvw_ref,
                  wbr_ref, wout_ref, *rest):
    cast_srcs, out_ref, cast_dsts = rest[:n_cast], rest[n_cast], rest[n_cast + 1:2 * n_cast + 1]
    tail_ref, hlru_ref, st_ref = rest[2 * n_cast + 1:]
    _cast_slabs(cast_srcs + cast_dsts)
    tt = h_ref.shape[1]
    n_chunks = tt // HGRN_CHUNK

    @pl.when(pl.program_id(1) == 0)
    def _():
        tail_ref[...] = jnp.zeros_like(tail_ref)
        hlru_ref[...] = jnp.zeros_like(hlru_ref)
        st_ref[...] = jnp.zeros_like(st_ref)

    row_blocks = [slice(i * tt // EDGE_BLOCKS, (i + 1) * tt // EDGE_BLOCKS) for i in range(EDGE_BLOCKS)]
    layer_row = lambda ref: ref[lb_layer:lb_layer + 1, :]
    u_blocks = [_rms(h_ref[0, rows, :], layer_row(gpre_ref)).astype(BF16) for rows in row_blocks]
    u = jnp.concatenate(u_blocks, axis=0)

    row = lax.broadcasted_iota(jnp.int32, (tt, W), 0)
    lane = lax.broadcasted_iota(jnp.int32, (tt, W), 1)
    sub = row & (SUBLANES - 1)
    n_groups = tt // SUBLANES

    gate_cols = D_MODEL // W
    gate_keys = lambda k: [("gate", k, j) for j in range(gate_cols)]
    pending = (gate_keys(0) + [("slot", 1)] + gate_keys(1) + gate_keys(2)
               + [("slot", s) for s in (6, 7, 8, 9)] + gate_keys(3))
    early = {}

    def issue(key):
        pending.remove(key)
        if key[0] == "slot":
            early[key] = _dot(u, win_ref[:, key[1] * W:(key[1] + 1) * W])
        else:
            lo = MIX_WIDTH + key[1] * D_MODEL + key[2] * W
            early[key] = jax.nn.sigmoid(_dot(u, win_ref[:, lo:lo + W]))

    def fill():
        if pending:
            issue(pending[0])

    def take(key):
        if key in pending:
            issue(key)
        return early.pop(key)

    def proj(slot):
        if slot == 0:
            w_slot = win_ref[:, 0:W]
            return jnp.concatenate([_dot(u_blk, w_slot) for u_blk in u_blocks], axis=0)
        if ("slot", slot) in pending or ("slot", slot) in early:
            return take(("slot", slot))
        return _dot(u, win_ref[:, slot * W:(slot + 1) * W])

    def merge(acc, k, y_k):
        gate = jnp.concatenate([take(key) for key in gate_keys(k)], axis=1)
        return acc + gate * _dot(y_k.astype(BF16), wbr_ref[k * W:(k + 1) * W, :])

    a_x = proj(0)
    ext = jnp.concatenate([tail_ref[0], a_x], axis=0)
    tail_ref[0] = a_x[tt - TAIL:]
    xc = layer_row(convb_ref) + sum(convw_ref[lb_layer, k:k + 1, :] * _shift_rows(ext, LRU_CONV - 1 - k, tt)
                                    for k in range(LRU_CONV))
    fill()
    xc_b = xc.astype(BF16)
    r = jax.nn.sigmoid(_dot(xc_b, wa_ref[lb_layer]) + layer_row(ba_ref))
    i_gate = jax.nn.sigmoid(_dot(xc_b, wx_ref[lb_layer]) + layer_row(bx_ref))
    log_a = (-LRU_C) * r * jax.nn.softplus(-layer_row(lam_ref))
    a = jnp.exp(log_a)
    mult = jnp.sqrt(jnp.tanh(-log_a) * (1.0 + a * a))
    uu = mult * i_gate * xc
    fill()
    s = 1
    while s < SUBLANES:
        keep = sub >= s
        a_sh = jnp.where(keep, _group_roll(a, s), 1.0)
        u_sh = jnp.where(keep, _group_roll(uu, s), 0.0)
        uu = uu + a * u_sh
        a = a * a_sh
        s *= 2
    fill()
    a_g = a.reshape(n_groups, SUBLANES, W)
    u_g = uu.reshape(n_groups, SUBLANES, W)
    h_prev = hlru_ref[0:1, :]
    h_groups = []
    for g in range(n_groups):
        h_g = u_g[g] + a_g[g] * h_prev
        h_groups.append(h_g)
        h_prev = h_g[SUBLANES - 1:SUBLANES, :]
    hlru_ref[0:1, :] = h_prev
    h_seq = jnp.concatenate(h_groups, axis=0)
    y_a = h_seq * jax.nn.gelu(proj(1))
    merged = merge(jnp.zeros((tt, D_MODEL), F32), 0, y_a)

    p_x = proj(2)
    ext = jnp.concatenate([tail_ref[1], p_x], axis=0)
    tail_ref[1] = p_x[tt - TAIL:]
    group = lax.shift_right_logical(lane, POOL_GROUP.bit_length() - 1)
    win_sum = ext
    mean = jnp.zeros((tt, W), F32)
    pos = (pl.program_id(1) * tt + row + 1).astype(F32)
    for g, win in enumerate(POOL_WINDOWS):
        win_sum = win_sum + pltpu.roll(win_sum, win // 2, 0)
        mean = jnp.where(group == g, win_sum[TAIL:] / jnp.minimum(pos, float(win)), mean)
    fill()
    y_b = _dot((mean - p_x).astype(BF16), poolw_ref[lb_layer]) * layer_row(pools_ref)
    merged = merge(merged, 1, y_b)

    c_q = proj(3)
    c_f = proj(4)
    v = proj(5)
    lb_sm = jax.nn.softmax(lbraw_ref[...], axis=0)
    lb = jnp.sum(lb_sm[1:lb_layer + 1], axis=0, keepdims=True) if lb_layer else jnp.zeros((1, W), F32)
    f = lb + (1.0 - lb) * jax.nn.sigmoid(c_f)
    lf = jnp.log(f)
    kk = 1.0 - f
    b = lf
    s = 1
    while s < SUBLANES:
        b = b + jnp.where(sub >= s, _group_roll(b, s), 0.0)
        s *= 2
    b_g = b.reshape(n_groups, SUBLANES, W)
    b_parts = []
    for g in range(n_groups):
        if g % (HGRN_CHUNK // SUBLANES) == 0:
            b_parts.append(b_g[g])
        else:
            b_parts.append(b_g[g] + b_parts[-1][SUBLANES - 1:SUBLANES, :])
    b = jnp.concatenate(b_parts, axis=0)
    fill()
    b3 = b.reshape(n_chunks, HGRN_CHUNK, W)
    b_last = b3[:, HGRN_CHUNK - 1:HGRN_CHUNK, :]
    qe_b = (c_q * jnp.exp(b)).astype(BF16)
    kd_b = (kk.reshape(n_chunks, HGRN_CHUNK, W) * jnp.exp(b_last - b3)).reshape(tt, W).astype(BF16)
    decay_chunk = jnp.exp(b_last)
    v_b = v.astype(BF16)

    sq_r = lax.broadcasted_iota(jnp.int32, (W, W), 0)
    sq_c = lax.broadcasted_iota(jnp.int32, (W, W), 1)
    same_head = (sq_r ^ sq_c) < HGRN_DK
    head_ones = jnp.where(same_head, 1.0, 0.0).astype(BF16)

    half_rows = HGRN_CHUNK // 2
    groups_per_chunk = HGRN_CHUNK // SUBLANES
    levels = []
    half = SUBLANES
    while half < HGRN_CHUNK:
        blk = 2 * half
        bb = b.reshape(tt // blk, blk, W)
        delta = (bb - bb[:, half - 1:half, :]).reshape(tt, W)
        levels.append((half, c_q * jnp.exp(jnp.minimum(delta, 0.0)), kk * jnp.exp(jnp.minimum(-delta, 0.0))))
        half = blk

    def half_groups(c, half, second):
        return [c * groups_per_chunk + g for g in range(groups_per_chunk)
                if ((g * SUBLANES) & half != 0) == second]

    def gather(x, groups):
        return jnp.concatenate([x[g * SUBLANES:(g + 1) * SUBLANES] for g in groups], axis=0).astype(BF16)

    stack_r = lax.broadcasted_iota(jnp.int32, (HGRN_HEADS * half_rows, W), 0)
    stack_c = lax.broadcasted_iota(jnp.int32, (HGRN_HEADS * half_rows, W), 1)
    stack_same_head = (stack_r // half_rows) == (stack_c // HGRN_DK)

    def head_stack(x_rows):
        return jnp.where(stack_same_head, jnp.concatenate([x_rows] * HGRN_HEADS, axis=0), 0.0)

    pair_t = lax.broadcasted_iota(jnp.int32, (half_rows, HGRN_HEADS * half_rows), 0)
    pair_s = lax.broadcasted_iota(jnp.int32, (half_rows, HGRN_HEADS * half_rows), 1) % half_rows

    chunk_rows = [slice(c * HGRN_CHUNK, (c + 1) * HGRN_CHUNK) for c in range(n_chunks)]
    updates = []
    for c, sl in enumerate(chunk_rows):
        updates.append(_dot_tn(v_b[sl], kd_b[sl]))
        if c % 4 == 3:
            fill()
    n_small = 0
    probs = {}
    for c in range(n_chunks):
        for half, q_fac, k_fac in levels:
            scores = _dot_nt(gather(q_fac, half_groups(c, half, True)),
                             head_stack(gather(k_fac, half_groups(c, half, False))))
            if 2 * half < HGRN_CHUNK:
                scores = jnp.where((pair_t // half) == (pair_s // half), scores, 0.0)
            probs[c, half] = scores.astype(BF16)
            n_small += 1
            if n_small % 4 == 0:
                fill()
    intra = [[None] * groups_per_chunk for _ in range(n_chunks)]
    for c in range(n_chunks):
        for half, _, _ in levels:
            o_l = _dot(probs[c, half], head_stack(gather(v, half_groups(c, half, False))))
            for i, g in enumerate(half_groups(c, half, True)):
                part = o_l[i * SUBLANES:(i + 1) * SUBLANES]
                g_in = g - c * groups_per_chunk
                intra[c][g_in] = part if intra[c][g_in] is None else intra[c][g_in] + part
            n_small += 1
            if n_small % 6 == 0:
                fill()

    assert n_chunks == SUBLANES
    st = st_ref[...]
    o_parts = []
    o_same = None
    decay = None
    f_d = f
    for d, sl in enumerate(chunk_rows):
        o_c = _dot_nt(qe_b[sl], st.astype(BF16))
        o_parts.append(o_c + jnp.concatenate(
            [jnp.zeros((SUBLANES, W), F32) if part is None else part for part in intra[d]], axis=0))
        st = st * decay_chunk[d] + jnp.where(same_head, updates[d], 0.0)
        if d == 0:
            w_d, v_d = c_q * kk, v
        else:
            decay = f if d == 1 else decay * f_d
            f_d = _group_roll(f, d)
            w_d = jnp.where(sub >= d, c_q * (1.0 - f_d) * decay, 0.0)
            v_d = _group_roll(v, d)
        term = _dot(w_d.astype(BF16), head_ones) * v_d
        o_same = term if o_same is None else o_same + term
        fill()
    st_ref[...] = st
    o = jnp.concatenate(o_parts, axis=0) + o_same

    ms = _dot((o * o).astype(BF16), head_ones) * (1.0 / HGRN_DK)
    y_c = o * lax.rsqrt(ms + EPS) * layer_row(hnorm_ref) * jax.nn.silu(proj(6))
    merged = merge(merged, 2, y_c)

    s_b = proj(7)
    z = proj(8) * proj(9)
    ext = jnp.concatenate([tail_ref[2], z], axis=0)
    tail_ref[2] = z[tt - TAIL:]
    y_d = s_b * sum(sconvw_ref[lb_layer, k:k + 1, :] * _shift_rows(ext, SCONV_WIDTH - 1 - k, tt)
                    for k in range(SCONV_WIDTH))

    merged = merge(merged, 3, y_d)
    merged_b = merged.astype(BF16)
    for rows in (slice(0, tt // 2), slice(tt // 2, tt)):
        mix = _dot(merged_b[rows], wout_ref[...])
        out_ref[0, rows, :] = h_ref[0, rows, :] + _rms(mix, layer_row(gpost_ref))


def _mlp_kernel(layer, n_cast, h_ref, gpre_ref, gpost_ref, wup_ref, wdown_ref, *rest):
    cast_srcs, out_ref, cast_dsts = rest[:n_cast], rest[n_cast], rest[n_cast + 1:]
    _cast_slabs(cast_srcs + cast_dsts)
    gain_pre, gain_post = gpre_ref[layer:layer + 1, :], gpost_ref[layer:layer + 1, :]
    n_rows = h_ref.shape[0]
    for i in range(MLP_ROW_BLOCKS):
        rows = slice(i * n_rows // MLP_ROW_BLOCKS, (i + 1) * n_rows // MLP_ROW_BLOCKS)
        h_in = h_ref[rows, :]
        u = _rms(h_in, gain_pre).astype(BF16)
        m = jnp.zeros(h_in.shape, F32)
        for j in range(D_FF // FF_CHUNK):
            cols = slice(j * FF_CHUNK, (j + 1) * FF_CHUNK)
            hid = jnp.square(jnp.maximum(_dot(u, wup_ref[:, cols]), 0.0))
            m = m + _dot(hid.astype(BF16), wdown_ref[cols, :])
        out_ref[rows, :] = h_in + _rms(m, gain_post)


def _whole(arr):
    nd = arr.ndim
    return pl.BlockSpec(arr.shape, lambda *_: (0,) * nd, pipeline_mode=pl.Buffered(1))


def _cast_specs(sources, layer, n_steps, step_of):
    in_specs, out_specs, out_shapes = [], [], []
    for src in sources:
        _, rows, cols = src.shape
        slab = rows // n_steps
        in_specs.append(pl.BlockSpec((None, slab, cols), lambda *g: (layer, step_of(*g), 0)))
        out_specs.append(pl.BlockSpec((slab, cols), lambda *g: (step_of(*g), 0)))
        out_shapes.append(jax.ShapeDtypeStruct((rows, cols), BF16))
    return in_specs, out_specs, out_shapes


def _block_diag(blocks):
    depth, n, bi, bo = blocks.shape
    eye = jnp.eye(n, dtype=blocks.dtype)
    return jnp.einsum("lnio,nm->lnimo", blocks, eye).reshape(depth, n * bi, n * bo).astype(BF16)


def _mixer_layer(h, layer, params, weights, cast_sources):
    bsz, seq, _ = h.shape
    n_t = seq // TIME_TILE
    w_in_b, w_branch_b, w_out_b = weights
    operands = [
        h,
        params["norm_mix_pre"], params["norm_mix_post"],
        w_in_b,
        params["lru_conv_w"], params["lru_conv_b"],
        params["lru_w_a_dense"], params["lru_b_a"],
        params["lru_w_x_dense"], params["lru_b_x"],
        params["lru_lambda"],
        params["pool_w_dense"], params["pool_scale"],
        params["hgrn_lower_bound"],
        params["hgrn_norm_heads"],
        params["sconv_w"],
        w_branch_b,
        w_out_b,
    ]
    cast_in, cast_out, cast_shapes = _cast_specs(cast_sources, layer, bsz * n_t, lambda bi, ti: bi * n_t + ti)
    in_specs = [pl.BlockSpec((1, TIME_TILE, D_MODEL), lambda bi, ti: (bi, ti, 0))]
    in_specs += [_whole(op) for op in operands[1:]] + cast_in
    h_out, *cast = pl.pallas_call(
        functools.partial(_mixer_kernel, layer, len(cast_sources)),
        grid=(bsz, n_t),
        in_specs=in_specs,
        out_specs=[pl.BlockSpec((1, TIME_TILE, D_MODEL), lambda bi, ti: (bi, ti, 0))] + cast_out,
        out_shape=[jax.ShapeDtypeStruct(h.shape, h.dtype)] + cast_shapes,
        scratch_shapes=[
            pltpu.VMEM((3, TAIL, W), F32),
            pltpu.VMEM((8, W), F32),
            pltpu.VMEM((W, W), F32),
        ],
        compiler_params=pltpu.CompilerParams(
            dimension_semantics=("parallel", "arbitrary"),
            vmem_limit_bytes=VMEM_LIMIT_BYTES),
        name=f"mixer_l{layer}",
    )(*operands, *cast_sources)
    return h_out, cast


def _mlp_layer(h, layer, params, weights, cast_sources, cast_layer):
    bsz, seq, d = h.shape
    h2 = h.reshape(bsz * seq, d)
    n_steps = bsz * seq // MLP_TILE
    operands = [h2, params["norm_mlp_pre"], params["norm_mlp_post"], *weights]
    cast_in, cast_out, cast_shapes = _cast_specs(cast_sources, cast_layer, n_steps, lambda i: i)
    in_specs = [pl.BlockSpec((MLP_TILE, d), lambda i: (i, 0))] + [_whole(op) for op in operands[1:]] + cast_in
    out, *cast = pl.pallas_call(
        functools.partial(_mlp_kernel, layer, len(cast_sources)),
        grid=(n_steps,),
        in_specs=in_specs,
        out_specs=[pl.BlockSpec((MLP_TILE, d), lambda i: (i, 0))] + cast_out,
        out_shape=[jax.ShapeDtypeStruct(h2.shape, h2.dtype)] + cast_shapes,
        compiler_params=pltpu.CompilerParams(
            dimension_semantics=("parallel",),
            vmem_limit_bytes=VMEM_LIMIT_BYTES),
        name=f"mlp_l{layer}",
    )(*operands, *cast_sources)
    return out.reshape(bsz, seq, d), cast


def kernel(x, norm_mix_pre, norm_mix_post, norm_mlp_pre, norm_mlp_post, w_in, lru_conv_w, lru_conv_b,
           lru_w_a, lru_b_a, lru_w_x, lru_b_x, lru_lambda, pool_w, pool_scale, hgrn_lower_bound,
           hgrn_norm, sconv_w, w_branch, w_out, w_up, w_down):
    params = dict(
        norm_mix_pre=norm_mix_pre, norm_mix_post=norm_mix_post, norm_mlp_pre=norm_mlp_pre,
        norm_mlp_post=norm_mlp_post, lru_conv_w=lru_conv_w, lru_conv_b=lru_conv_b,
        lru_w_a_dense=_block_diag(lru_w_a), lru_b_a=lru_b_a, lru_w_x_dense=_block_diag(lru_w_x),
        lru_b_x=lru_b_x, lru_lambda=lru_lambda, pool_w_dense=_block_diag(pool_w), pool_scale=pool_scale,
        hgrn_lower_bound=hgrn_lower_bound, hgrn_norm_heads=jnp.tile(hgrn_norm, (1, HGRN_HEADS)),
        sconv_w=sconv_w)
    depth = w_in.shape[0]
    w_branch2 = w_branch.reshape(depth, N_BRANCHES * W, D_MODEL)
    mixer_sources = (w_in, w_branch2, w_out)
    mlp_sources = (w_up, w_down)
    mixer_weights = [w[0].astype(BF16) for w in mixer_sources]
    h = x
    for layer in range(depth):
        h, mlp_weights = _mixer_layer(h, layer, params, mixer_weights, mlp_sources)
        last = layer + 1 == depth
        h, mixer_weights = _mlp_layer(h, layer, params, mlp_weights,
                                      () if last else mixer_sources, layer + 1)
    return h
```

```python
import functools

import jax
import jax.numpy as jnp
from jax import lax
from jax.experimental import pallas as pl
from jax.experimental.pallas import tpu as pltpu

D_MODEL = 1024
N_BRANCHES = 4
W = D_MODEL // N_BRANCHES
LRU_CONV = 4
LRU_C = 8.0
POOL_WINDOWS = (2, 4, 8, 16)
POOL_GROUP = W // len(POOL_WINDOWS)
HGRN_HEADS = 4
HGRN_DK = W // HGRN_HEADS
SCONV_WIDTH = 3
D_FF = 4 * D_MODEL
N_MIX_SLOTS = 10
MIX_WIDTH = N_MIX_SLOTS * W
EPS = 1e-6

TIME_TILE = 512
HGRN_CHUNK = 64
SUBLANES = 8
EDGE_BLOCKS = 4
TAIL = 16
MLP_TILE = 1024
MLP_ROW_BLOCKS = 2
FF_CHUNK = 2048
VMEM_LIMIT_BYTES = 56 * 1024 * 1024

F32 = jnp.float32
BF16 = jnp.bfloat16


def _dot(a, b):
    return jnp.dot(a, b, preferred_element_type=F32)


def _dot_nt(a, b):
    return lax.dot_general(a, b, (((1,), (1,)), ((), ())), preferred_element_type=F32)


def _dot_tn(a, b):
    return lax.dot_general(a, b, (((0,), (0,)), ((), ())), preferred_element_type=F32)


def _rms(x, gain):
    return x * lax.rsqrt(jnp.mean(x * x, axis=-1, keepdims=True) + EPS) * gain


def _group_roll(x, s):
    n, w = x.shape
    return pltpu.roll(x.reshape(n // SUBLANES, SUBLANES, w), s, 1).reshape(n, w)


def _shift_rows(ext, s, n):
    if s == 0:
        return ext[TAIL:TAIL + n]
    return pltpu.roll(ext, s, 0)[TAIL:TAIL + n]


def _cast_slabs(refs):
    n = len(refs) // 2
    for src_ref, dst_ref in zip(refs[:n], refs[n:]):
        dst_ref[...] = src_ref[...].astype(BF16)


def _mixer_kernel(lb_layer, n_cast, h_ref, gpre_ref, gpost_ref, win_ref, convw_ref, convb_ref, wa_ref,
                  ba_ref, wx_ref, bx_ref, lam_ref, poolw_ref, pools_ref, lbraw_ref, hnorm_ref, sconvw_ref,
                  wbr_ref, wout_ref, *rest):
    cast_srcs, out_ref, cast_dsts = rest[:n_cast], rest[n_cast], rest[n_cast + 1:2 * n_cast + 1]
    tail_ref, hlru_ref, st_ref = rest[2 * n_cast + 1:]
    _cast_slabs(cast_srcs + cast_dsts)
    tt = h_ref.shape[1]
    n_chunks = tt // HGRN_CHUNK

    @pl.when(pl.program_id(1) == 0)
    def _():
        tail_ref[...] = jnp.zeros_like(tail_ref)
        hlru_ref[...] = jnp.zeros_like(hlru_ref)
        st_ref[...] = jnp.zeros_like(st_ref)

    row_blocks = [slice(i * tt // EDGE_BLOCKS, (i + 1) * tt // EDGE_BLOCKS) for i in range(EDGE_BLOCKS)]
    layer_row = lambda ref: ref[lb_layer:lb_layer + 1, :]
    u_blocks = [_rms(h_ref[0, rows, :], layer_row(gpre_ref)).astype(BF16) for rows in row_blocks]
    u = jnp.concatenate(u_blocks, axis=0)

    row = lax.broadcasted_iota(jnp.int32, (tt, W), 0)
    lane = lax.broadcasted_iota(jnp.int32, (tt, W), 1)
    sub = row & (SUBLANES - 1)
    n_groups = tt // SUBLANES

    gate_cols = D_MODEL // W
    gate_keys = lambda k: [("gate", k, j) for j in range(gate_cols)]
    pending = (gate_keys(0) + [("slot", 1)] + gate_keys(1) + gate_keys(2)
               + [("slot", s) for s in (6, 7, 8, 9)] + gate_keys(3))
    early = {}

    def issue(key):
        pending.remove(key)
        if key[0] == "slot":
            early[key] = _dot(u, win_ref[:, key[1] * W:(key[1] + 1) * W])
        else:
            lo = MIX_WIDTH + key[1] * D_MODEL + key[2] * W
            early[key] = jax.nn.sigmoid(_dot(u, win_ref[:, lo:lo + W]))

    def fill():
        if pending:
            issue(pending[0])

    def take(key):
        if key in pending:
            issue(key)
        return early.pop(key)

    def proj(slot):
        if slot == 0:
            w_slot = win_ref[:, 0:W]
            return jnp.concatenate([_dot(u_blk, w_slot) for u_blk in u_blocks], axis=0)
        if ("slot", slot) in pending or ("slot", slot) in early:
            return take(("slot", slot))
        return _dot(u, win_ref[:, slot * W:(slot + 1) * W])

    def merge(acc, k, y_k):
        gate = jnp.concatenate([take(key) for key in gate_keys(k)], axis=1)
        return acc + gate * _dot(y_k.astype(BF16), wbr_ref[k * W:(k + 1) * W, :])

    a_x = proj(0)
    ext = jnp.concatenate([tail_ref[0], a_x], axis=0)
    tail_ref[0] = a_x[tt - TAIL:]
    xc = layer_row(convb_ref) + sum(convw_ref[lb_layer, k:k + 1, :] * _shift_rows(ext, LRU_CONV - 1 - k, tt)
                                    for k in range(LRU_CONV))
    fill()
    xc_b = xc.astype(BF16)
    r = jax.nn.sigmoid(_dot(xc_b, wa_ref[lb_layer]) + layer_row(ba_ref))
    i_gate = jax.nn.sigmoid(_dot(xc_b, wx_ref[lb_layer]) + layer_row(bx_ref))
    log_a = (-LRU_C) * r * jax.nn.softplus(-layer_row(lam_ref))
    a = jnp.exp(log_a)
    mult = jnp.sqrt(jnp.tanh(-log_a) * (1.0 + a * a))
    uu = mult * i_gate * xc
    fill()
    s = 1
    while s < SUBLANES:
        keep = sub >= s
        a_sh = jnp.where(keep, _group_roll(a, s), 1.0)
        u_sh = jnp.where(keep, _group_roll(uu, s), 0.0)
        uu = uu + a * u_sh
        a = a * a_sh
        s *= 2
    fill()
    a_g = a.reshape(n_groups, SUBLANES, W)
    u_g = uu.reshape(n_groups, SUBLANES, W)
    h_prev = hlru_ref[0:1, :]
    h_groups = []
    for g in range(n_groups):
        h_g = u_g[g] + a_g[g] * h_prev
        h_groups.append(h_g)
        h_prev = h_g[SUBLANES - 1:SUBLANES, :]
    hlru_ref[0:1, :] = h_prev
    h_seq = jnp.concatenate(h_groups, axis=0)
    y_a = h_seq * jax.nn.gelu(proj(1))
    merged = merge(jnp.zeros((tt, D_MODEL), F32), 0, y_a)

    p_x = proj(2)
    ext = jnp.concatenate([tail_ref[1], p_x], axis=0)
    tail_ref[1] = p_x[tt - TAIL:]
    group = lax.shift_right_logical(lane, POOL_GROUP.bit_length() - 1)
    win_sum = ext
    mean = jnp.zeros((tt, W), F32)
    pos = (pl.program_id(1) * tt + row + 1).astype(F32)
    for g, win in enumerate(POOL_WINDOWS):
        win_sum = win_sum + pltpu.roll(win_sum, win // 2, 0)
        mean = jnp.where(group == g, win_sum[TAIL:] / jnp.minimum(pos, float(win)), mean)
    fill()
    y_b = _dot((mean - p_x).astype(BF16), poolw_ref[lb_layer]) * layer_row(pools_ref)
    merged = merge(merged, 1, y_b)

    s_b = proj(7)
    z = proj(8) * proj(9)
    ext = jnp.concatenate([tail_ref[2], z], axis=0)
    tail_ref[2] = z[tt - TAIL:]
    y_d = s_b * sum(sconvw_ref[lb_layer, k:k + 1, :] * _shift_rows(ext, SCONV_WIDTH - 1 - k, tt)
                    for k in range(SCONV_WIDTH))

    c_q = proj(3)
    c_f = proj(4)
    v = proj(5)
    lb_sm = jax.nn.softmax(lbraw_ref[...], axis=0)
    lb = jnp.sum(lb_sm[1:lb_layer + 1], axis=0, keepdims=True) if lb_layer else jnp.zeros((1, W), F32)
    f = lb + (1.0 - lb) * jax.nn.sigmoid(c_f)
    lf = jnp.log(f)
    kk = 1.0 - f
    b = lf
    s = 1
    while s < SUBLANES:
        b = b + jnp.where(sub >= s, _group_roll(b, s), 0.0)
        s *= 2
    b_g = b.reshape(n_groups, SUBLANES, W)
    b_parts = []
    for g in range(n_groups):
        if g % (HGRN_CHUNK // SUBLANES) == 0:
            b_parts.append(b_g[g])
        else:
            b_parts.append(b_g[g] + b_parts[-1][SUBLANES - 1:SUBLANES, :])
    b = jnp.concatenate(b_parts, axis=0)
    fill()
    b3 = b.reshape(n_chunks, HGRN_CHUNK, W)
    b_last = b3[:, HGRN_CHUNK - 1:HGRN_CHUNK, :]
    qe_b = (c_q * jnp.exp(b)).astype(BF16)
    kd_b = (kk.reshape(n_chunks, HGRN_CHUNK, W) * jnp.exp(b_last - b3)).reshape(tt, W).astype(BF16)
    decay_chunk = jnp.exp(b_last)
    v_b = v.astype(BF16)

    sq_r = lax.broadcasted_iota(jnp.int32, (W, W), 0)
    sq_c = lax.broadcasted_iota(jnp.int32, (W, W), 1)
    same_head = (sq_r ^ sq_c) < HGRN_DK
    head_ones = jnp.where(same_head, 1.0, 0.0).astype(BF16)

    half_rows = HGRN_CHUNK // 2
    groups_per_chunk = HGRN_CHUNK // SUBLANES
    levels = []
    half = SUBLANES
    while half < HGRN_CHUNK:
        blk = 2 * half
        bb = b.reshape(tt // blk, blk, W)
        delta = (bb - bb[:, half - 1:half, :]).reshape(tt, W)
        levels.append((half, c_q * jnp.exp(jnp.minimum(delta, 0.0)), kk * jnp.exp(jnp.minimum(-delta, 0.0))))
        half = blk

    def half_groups(c, half, second):
        return [c * groups_per_chunk + g for g in range(groups_per_chunk)
                if ((g * SUBLANES) & half != 0) == second]

    def gather(x, groups):
        return jnp.concatenate([x[g * SUBLANES:(g + 1) * SUBLANES] for g in groups], axis=0).astype(BF16)

    stack_r = lax.broadcasted_iota(jnp.int32, (HGRN_HEADS * half_rows, W), 0)
    stack_c = lax.broadcasted_iota(jnp.int32, (HGRN_HEADS * half_rows, W), 1)
    stack_same_head = (stack_r // half_rows) == (stack_c // HGRN_DK)

    def head_stack(x_rows):
        return jnp.where(stack_same_head, jnp.concatenate([x_rows] * HGRN_HEADS, axis=0), 0.0)

    pair_t = lax.broadcasted_iota(jnp.int32, (half_rows, HGRN_HEADS * half_rows), 0)
    pair_s = lax.broadcasted_iota(jnp.int32, (half_rows, HGRN_HEADS * half_rows), 1) % half_rows

    chunk_rows = [slice(c * HGRN_CHUNK, (c + 1) * HGRN_CHUNK) for c in range(n_chunks)]
    updates = []
    for c, sl in enumerate(chunk_rows):
        updates.append(_dot_tn(v_b[sl], kd_b[sl]))
        if c % 4 == 3:
            fill()
    n_small = 0
    probs = {}
    for c in range(n_chunks):
        for half, q_fac, k_fac in levels:
            scores = _dot_nt(gather(q_fac, half_groups(c, half, True)),
                             head_stack(gather(k_fac, half_groups(c, half, False))))
            if 2 * half < HGRN_CHUNK:
                scores = jnp.where((pair_t // half) == (pair_s // half), scores, 0.0)
            probs[c, half] = scores.astype(BF16)
            n_small += 1
            if n_small % 4 == 0:
                fill()
    intra = [[None] * groups_per_chunk for _ in range(n_chunks)]
    for c in range(n_chunks):
        for half, _, _ in levels:
            o_l = _dot(probs[c, half], head_stack(gather(v, half_groups(c, half, False))))
            for i, g in enumerate(half_groups(c, half, True)):
                part = o_l[i * SUBLANES:(i + 1) * SUBLANES]
                g_in = g - c * groups_per_chunk
                intra[c][g_in] = part if intra[c][g_in] is None else intra[c][g_in] + part
            n_small += 1
            if n_small % 6 == 0:
                fill()

    merged = merge(merged, 3, y_d)

    assert n_chunks == SUBLANES
    st = st_ref[...]
    o_parts = []
    o_same = None
    decay = None
    f_d = f
    for d, sl in enumerate(chunk_rows):
        o_c = _dot_nt(qe_b[sl], st.astype(BF16))
        o_parts.append(o_c + jnp.concatenate(
            [jnp.zeros((SUBLANES, W), F32) if part is None else part for part in intra[d]], axis=0))
        st = st * decay_chunk[d] + jnp.where(same_head, updates[d], 0.0)
        if d == 0:
            w_d, v_d = c_q * kk, v
        else:
            decay = f if d == 1 else decay * f_d
            f_d = _group_roll(f, d)
            w_d = jnp.where(sub >= d, c_q * (1.0 - f_d) * decay, 0.0)
            v_d = _group_roll(v, d)
        term = _dot(w_d.astype(BF16), head_ones) * v_d
        o_same = term if o_same is None else o_same + term
        fill()
    st_ref[...] = st
    o = jnp.concatenate(o_parts, axis=0) + o_same

    ms = _dot((o * o).astype(BF16), head_ones) * (1.0 / HGRN_DK)
    y_c = o * lax.rsqrt(ms + EPS) * layer_row(hnorm_ref) * jax.nn.silu(proj(6))
    merged = merge(merged, 2, y_c)

    merged_b = merged.astype(BF16)
    for rows in (slice(0, tt // 2), slice(tt // 2, tt)):
        mix = _dot(merged_b[rows], wout_ref[...])
        out_ref[0, rows, :] = h_ref[0, rows, :] + _rms(mix, layer_row(gpost_ref))


def _mlp_kernel(layer, n_cast, h_ref, gpre_ref, gpost_ref, wup_ref, wdown_ref, *rest):
    cast_srcs, out_ref, cast_dsts = rest[:n_cast], rest[n_cast], rest[n_cast + 1:]
    _cast_slabs(cast_srcs + cast_dsts)
    gain_pre, gain_post = gpre_ref[layer:layer + 1, :], gpost_ref[layer:layer + 1, :]
    n_rows = h_ref.shape[0]
    for i in range(MLP_ROW_BLOCKS):
        rows = slice(i * n_rows // MLP_ROW_BLOCKS, (i + 1) * n_rows // MLP_ROW_BLOCKS)
        h_in = h_ref[rows, :]
        u = _rms(h_in, gain_pre).astype(BF16)
        m = jnp.zeros(h_in.shape, F32)
        for j in range(D_FF // FF_CHUNK):
            cols = slice(j * FF_CHUNK, (j + 1) * FF_CHUNK)
            hid = jnp.square(jnp.maximum(_dot(u, wup_ref[:, cols]), 0.0))
            m = m + _dot(hid.astype(BF16), wdown_ref[cols, :])
        out_ref[rows, :] = h_in + _rms(m, gain_post)


def _whole(arr):
    nd = arr.ndim
    return pl.BlockSpec(arr.shape, lambda *_: (0,) * nd, pipeline_mode=pl.Buffered(1))


def _cast_specs(sources, layer, n_steps, step_of):
    in_specs, out_specs, out_shapes = [], [], []
    for src in sources:
        _, rows, cols = src.shape
        slab = rows // n_steps
        in_specs.append(pl.BlockSpec((None, slab, cols), lambda *g: (layer, step_of(*g), 0)))
        out_specs.append(pl.BlockSpec((slab, cols), lambda *g: (step_of(*g), 0)))
        out_shapes.append(jax.ShapeDtypeStruct((rows, cols), BF16))
    return in_specs, out_specs, out_shapes


def _block_diag(blocks):
    depth, n, bi, bo = blocks.shape
    eye = jnp.eye(n, dtype=blocks.dtype)
    return jnp.einsum("lnio,nm->lnimo", blocks, eye).reshape(depth, n * bi, n * bo).astype(BF16)


def _mixer_layer(h, layer, params, weights, cast_sources):
    bsz, seq, _ = h.shape
    n_t = seq // TIME_TILE
    w_in_b, w_branch_b, w_out_b = weights
    operands = [
        h,
        params["norm_mix_pre"], params["norm_mix_post"],
        w_in_b,
        params["lru_conv_w"], params["lru_conv_b"],
        params["lru_w_a_dense"], params["lru_b_a"],
        params["lru_w_x_dense"], params["lru_b_x"],
        params["lru_lambda"],
        params["pool_w_dense"], params["pool_scale"],
        params["hgrn_lower_bound"],
        params["hgrn_norm_heads"],
        params["sconv_w"],
        w_branch_b,
        w_out_b,
    ]
    cast_in, cast_out, cast_shapes = _cast_specs(cast_sources, layer, bsz * n_t, lambda bi, ti: bi * n_t + ti)
    in_specs = [pl.BlockSpec((1, TIME_TILE, D_MODEL), lambda bi, ti: (bi, ti, 0))]
    in_specs += [_whole(op) for op in operands[1:]] + cast_in
    h_out, *cast = pl.pallas_call(
        functools.partial(_mixer_kernel, layer, len(cast_sources)),
        grid=(bsz, n_t),
        in_specs=in_specs,
        out_specs=[pl.BlockSpec((1, TIME_TILE, D_MODEL), lambda bi, ti: (bi, ti, 0))] + cast_out,
        out_shape=[jax.ShapeDtypeStruct(h.shape, h.dtype)] + cast_shapes,
        scratch_shapes=[
            pltpu.VMEM((3, TAIL, W), F32),
            pltpu.VMEM((8, W), F32),
            pltpu.VMEM((W, W), F32),
        ],
        compiler_params=pltpu.CompilerParams(
            dimension_semantics=("parallel", "arbitrary"),
            vmem_limit_bytes=VMEM_LIMIT_BYTES),
        name=f"mixer_l{layer}",
    )(*operands, *cast_sources)
    return h_out, cast


def _mlp_layer(h, layer, params, weights, cast_sources, cast_layer):
    bsz, seq, d = h.shape
    h2 = h.reshape(bsz * seq, d)
    n_steps = bsz * seq // MLP_TILE
    operands = [h2, params["norm_mlp_pre"], params["norm_mlp_post"], *weights]
    cast_in, cast_out, cast_shapes = _cast_specs(cast_sources, cast_layer, n_steps, lambda i: i)
    in_specs = [pl.BlockSpec((MLP_TILE, d), lambda i: (i, 0))] + [_whole(op) for op in operands[1:]] + cast_in
    out, *cast = pl.pallas_call(
        functools.partial(_mlp_kernel, layer, len(cast_sources)),
        grid=(n_steps,),
        in_specs=in_specs,
        out_specs=[pl.BlockSpec((MLP_TILE, d), lambda i: (i, 0))] + cast_out,
        out_shape=[jax.ShapeDtypeStruct(h2.shape, h2.dtype)] + cast_shapes,
        compiler_params=pltpu.CompilerParams(
            dimension_semantics=("parallel",),
            vmem_limit_bytes=VMEM_LIMIT_BYTES),
        name=f"mlp_l{layer}",
    )(*operands, *cast_sources)
    return out.reshape(bsz, seq, d), cast


def kernel(x, norm_mix_pre, norm_mix_post, norm_mlp_pre, norm_mlp_post, w_in, lru_conv_w, lru_conv_b,
           lru_w_a, lru_b_a, lru_w_x, lru_b_x, lru_lambda, pool_w, pool_scale, hgrn_lower_bound,
           hgrn_norm, sconv_w, w_branch, w_out, w_up, w_down):
    params = dict(
        norm_mix_pre=norm_mix_pre, norm_mix_post=norm_mix_post, norm_mlp_pre=norm_mlp_pre,
        norm_mlp_post=norm_mlp_post, lru_conv_w=lru_conv_w, lru_conv_b=lru_conv_b,
        lru_w_a_dense=_block_diag(lru_w_a), lru_b_a=lru_b_a, lru_w_x_dense=_block_diag(lru_w_x),
        lru_b_x=lru_b_x, lru_lambda=lru_lambda, pool_w_dense=_block_diag(pool_w), pool_scale=pool_scale,
        hgrn_lower_bound=hgrn_lower_bound, hgrn_norm_heads=jnp.tile(hgrn_norm, (1, HGRN_HEADS)),
        sconv_w=sconv_w)
    depth = w_in.shape[0]
    w_branch2 = w_branch.reshape(depth, N_BRANCHES * W, D_MODEL)
    mixer_sources = (w_in, w_branch2, w_out)
    mlp_sources = (w_up, w_down)
    mixer_weights = [w[0].astype(BF16) for w in mixer_sources]
    h = x
    for layer in range(depth):
        h, mlp_weights = _mixer_layer(h, layer, params, mixer_weights, mlp_sources)
        last = layer + 1 == depth
        h, mixer_weights = _mlp_layer(h, layer, params, mlp_weights,
                                      () if last else mixer_sources, layer + 1)
    return h
```

```python
import functools

import jax
import jax.numpy as jnp
from jax import lax
from jax.experimental import pallas as pl
from jax.experimental.pallas import tpu as pltpu

D_MODEL = 1024
N_BRANCHES = 4
W = D_MODEL // N_BRANCHES
LRU_CONV = 4
LRU_C = 8.0
POOL_WINDOWS = (2, 4, 8, 16)
POOL_GROUP = W // len(POOL_WINDOWS)
HGRN_HEADS = 4
HGRN_DK = W // HGRN_HEADS
SCONV_WIDTH = 3
D_FF = 4 * D_MODEL
N_MIX_SLOTS = 10
MIX_WIDTH = N_MIX_SLOTS * W
EPS = 1e-6

TIME_TILE = 512
HGRN_CHUNK = 64
SUBLANES = 8
EDGE_BLOCKS = 4
TAIL = 16
MLP_TILE = 1024
MLP_ROW_BLOCKS = 2
FF_CHUNK = 2048
VMEM_LIMIT_BYTES = 56 * 1024 * 1024

F32 = jnp.float32
BF16 = jnp.bfloat16


def _dot(a, b):
    return jnp.dot(a, b, preferred_element_type=F32)


def _dot_nt(a, b):
    return lax.dot_general(a, b, (((1,), (1,)), ((), ())), preferred_element_type=F32)


def _dot_tn(a, b):
    return lax.dot_general(a, b, (((0,), (0,)), ((), ())), preferred_element_type=F32)


def _rms(x, gain):
    return x * lax.rsqrt(jnp.mean(x * x, axis=-1, keepdims=True) + EPS) * gain


def _group_roll(x, s):
    n, w = x.shape
    return pltpu.roll(x.reshape(n // SUBLANES, SUBLANES, w), s, 1).reshape(n, w)


def _shift_rows(ext, s, n):
    if s == 0:
        return ext[TAIL:TAIL + n]
    return pltpu.roll(ext, s, 0)[TAIL:TAIL + n]


def _cast_slabs(refs):
    n = len(refs) // 2
    for src_ref, dst_ref in zip(refs[:n], refs[n:]):
        dst_ref[...] = src_ref[...].astype(BF16)


def _mixer_kernel(lb_layer, n_cast, h_ref, gpre_ref, gpost_ref, win_ref, convw_ref, convb_ref, wa_ref,
                  ba_ref, wx_ref, bx_ref, lam_ref, poolw_ref, pools_ref, lbraw_ref, hnorm_ref, sconvw_ref,
                  wbr_ref, wout_ref, *rest):
    cast_srcs, out_ref, cast_dsts = rest[:n_cast], rest[n_cast], rest[n_cast + 1:2 * n_cast + 1]
    tail_ref, hlru_ref, st_ref = rest[2 * n_cast + 1:]
    _cast_slabs(cast_srcs + cast_dsts)
    tt = h_ref.shape[1]
    n_chunks = tt // HGRN_CHUNK

    @pl.when(pl.program_id(1) == 0)
    def _():
        tail_ref[...] = jnp.zeros_like(tail_ref)
        hlru_ref[...] = jnp.zeros_like(hlru_ref)
        st_ref[...] = jnp.zeros_like(st_ref)

    row_blocks = [slice(i * tt // EDGE_BLOCKS, (i + 1) * tt // EDGE_BLOCKS) for i in range(EDGE_BLOCKS)]
    layer_row = lambda ref: ref[lb_layer:lb_layer + 1, :]
    u_blocks = [_rms(h_ref[0, rows, :], layer_row(gpre_ref)).astype(BF16) for rows in row_blocks]
    u = jnp.concatenate(u_blocks, axis=0)

    row = lax.broadcasted_iota(jnp.int32, (tt, W), 0)
    lane = lax.broadcasted_iota(jnp.int32, (tt, W), 1)
    sub = row & (SUBLANES - 1)
    n_groups = tt // SUBLANES

    gate_cols = D_MODEL // W
    gate_keys = lambda k: [("gate", k, j) for j in range(gate_cols)]
    pending = (gate_keys(0) + [("slot", 1)] + gate_keys(1) + gate_keys(2)
               + [("slot", s) for s in (6, 7, 8, 9)] + gate_keys(3))
    early = {}

    def issue(key):
        pending.remove(key)
        if key[0] == "slot":
            early[key] = _dot(u, win_ref[:, key[1] * W:(key[1] + 1) * W])
        else:
            lo = MIX_WIDTH + key[1] * D_MODEL + key[2] * W
            early[key] = 0.5 * jnp.tanh(0.5 * _dot(u, win_ref[:, lo:lo + W])) + 0.5

    def fill():
        if pending:
            issue(pending[0])

    def take(key):
        if key in pending:
            issue(key)
        return early.pop(key)

    def proj(slot):
        if slot == 0:
            w_slot = win_ref[:, 0:W]
            return jnp.concatenate([_dot(u_blk, w_slot) for u_blk in u_blocks], axis=0)
        if ("slot", slot) in pending or ("slot", slot) in early:
            return take(("slot", slot))
        return _dot(u, win_ref[:, slot * W:(slot + 1) * W])

    def merge(acc, k, y_k):
        gate = jnp.concatenate([take(key) for key in gate_keys(k)], axis=1)
        return acc + gate * _dot(y_k.astype(BF16), wbr_ref[k * W:(k + 1) * W, :])

    a_x = proj(0)
    ext = jnp.concatenate([tail_ref[0], a_x], axis=0)
    tail_ref[0] = a_x[tt - TAIL:]
    xc = layer_row(convb_ref) + sum(convw_ref[lb_layer, k:k + 1, :] * _shift_rows(ext, LRU_CONV - 1 - k, tt)
                                    for k in range(LRU_CONV))
    fill()
    xc_b = xc.astype(BF16)
    r = jax.nn.sigmoid(_dot(xc_b, wa_ref[lb_layer]) + layer_row(ba_ref))
    i_gate = jax.nn.sigmoid(_dot(xc_b, wx_ref[lb_layer]) + layer_row(bx_ref))
    log_a = (-LRU_C) * r * jax.nn.softplus(-layer_row(lam_ref))
    a = jnp.exp(log_a)
    mult = jnp.sqrt(jnp.tanh(-log_a) * (1.0 + a * a))
    uu = mult * i_gate * xc
    fill()
    s = 1
    while s < SUBLANES:
        keep = sub >= s
        a_sh = jnp.where(keep, _group_roll(a, s), 1.0)
        u_sh = jnp.where(keep, _group_roll(uu, s), 0.0)
        uu = uu + a * u_sh
        a = a * a_sh
        s *= 2
    fill()
    a_g = a.reshape(n_groups, SUBLANES, W)
    u_g = uu.reshape(n_groups, SUBLANES, W)
    h_prev = hlru_ref[0:1, :]
    h_groups = []
    for g in range(n_groups):
        h_g = u_g[g] + a_g[g] * h_prev
        h_groups.append(h_g)
        h_prev = h_g[SUBLANES - 1:SUBLANES, :]
    hlru_ref[0:1, :] = h_prev
    h_seq = jnp.concatenate(h_groups, axis=0)
    y_a = h_seq * jax.nn.gelu(proj(1))
    merged = merge(jnp.zeros((tt, D_MODEL), F32), 0, y_a)

    p_x = proj(2)
    ext = jnp.concatenate([tail_ref[1], p_x], axis=0)
    tail_ref[1] = p_x[tt - TAIL:]
    group = lax.shift_right_logical(lane, POOL_GROUP.bit_length() - 1)
    win_sum = ext
    mean = jnp.zeros((tt, W), F32)
    pos = (pl.program_id(1) * tt + row + 1).astype(F32)
    for g, win in enumerate(POOL_WINDOWS):
        win_sum = win_sum + pltpu.roll(win_sum, win // 2, 0)
        mean = jnp.where(group == g, win_sum[TAIL:] / jnp.minimum(pos, float(win)), mean)
    fill()
    y_b = _dot((mean - p_x).astype(BF16), poolw_ref[lb_layer]) * layer_row(pools_ref)
    merged = merge(merged, 1, y_b)

    s_b = proj(7)
    z = proj(8) * proj(9)
    ext = jnp.concatenate([tail_ref[2], z], axis=0)
    tail_ref[2] = z[tt - TAIL:]
    y_d = s_b * sum(sconvw_ref[lb_layer, k:k + 1, :] * _shift_rows(ext, SCONV_WIDTH - 1 - k, tt)
                    for k in range(SCONV_WIDTH))

    c_q = proj(3)
    c_f = proj(4)
    v = proj(5)
    lb_sm = jax.nn.softmax(lbraw_ref[...], axis=0)
    lb = jnp.sum(lb_sm[1:lb_layer + 1], axis=0, keepdims=True) if lb_layer else jnp.zeros((1, W), F32)
    f = lb + (1.0 - lb) * jax.nn.sigmoid(c_f)
    lf = jnp.log(f)
    kk = 1.0 - f
    b = lf
    s = 1
    while s < SUBLANES:
        b = b + jnp.where(sub >= s, _group_roll(b, s), 0.0)
        s *= 2
    b_g = b.reshape(n_groups, SUBLANES, W)
    b_parts = []
    for g in range(n_groups):
        if g % (HGRN_CHUNK // SUBLANES) == 0:
            b_parts.append(b_g[g])
        else:
            b_parts.append(b_g[g] + b_parts[-1][SUBLANES - 1:SUBLANES, :])
    b = jnp.concatenate(b_parts, axis=0)
    fill()
    b3 = b.reshape(n_chunks, HGRN_CHUNK, W)
    b_last = b3[:, HGRN_CHUNK - 1:HGRN_CHUNK, :]
    qe_b = (c_q * jnp.exp(b)).astype(BF16)
    kd_b = (kk.reshape(n_chunks, HGRN_CHUNK, W) * jnp.exp(b_last - b3)).reshape(tt, W).astype(BF16)
    decay_chunk = jnp.exp(b_last)
    v_b = v.astype(BF16)

    sq_r = lax.broadcasted_iota(jnp.int32, (W, W), 0)
    sq_c = lax.broadcasted_iota(jnp.int32, (W, W), 1)
    same_head = (sq_r ^ sq_c) < HGRN_DK
    head_ones = jnp.where(same_head, 1.0, 0.0).astype(BF16)

    half_rows = HGRN_CHUNK // 2
    groups_per_chunk = HGRN_CHUNK // SUBLANES
    levels = []
    half = SUBLANES
    while half < HGRN_CHUNK:
        blk = 2 * half
        bb = b.reshape(tt // blk, blk, W)
        delta = (bb - bb[:, half - 1:half, :]).reshape(tt, W)
        levels.append((half, c_q * jnp.exp(jnp.minimum(delta, 0.0)), kk * jnp.exp(jnp.minimum(-delta, 0.0))))
        half = blk

    def half_groups(c, half, second):
        return [c * groups_per_chunk + g for g in range(groups_per_chunk)
                if ((g * SUBLANES) & half != 0) == second]

    def gather(x, groups):
        return jnp.concatenate([x[g * SUBLANES:(g + 1) * SUBLANES] for g in groups], axis=0).astype(BF16)

    stack_r = lax.broadcasted_iota(jnp.int32, (HGRN_HEADS * half_rows, W), 0)
    stack_c = lax.broadcasted_iota(jnp.int32, (HGRN_HEADS * half_rows, W), 1)
    stack_same_head = (stack_r // half_rows) == (stack_c // HGRN_DK)

    def head_stack(x_rows):
        return jnp.where(stack_same_head, jnp.concatenate([x_rows] * HGRN_HEADS, axis=0), 0.0)

    pair_t = lax.broadcasted_iota(jnp.int32, (half_rows, HGRN_HEADS * half_rows), 0)
    pair_s = lax.broadcasted_iota(jnp.int32, (half_rows, HGRN_HEADS * half_rows), 1) % half_rows

    chunk_rows = [slice(c * HGRN_CHUNK, (c + 1) * HGRN_CHUNK) for c in range(n_chunks)]
    updates = []
    for c, sl in enumerate(chunk_rows):
        updates.append(_dot_tn(v_b[sl], kd_b[sl]))
        if c % 4 == 3:
            fill()
    n_small = 0
    probs = {}
    for c in range(n_chunks):
        for half, q_fac, k_fac in levels:
            scores = _dot_nt(gather(q_fac, half_groups(c, half, True)),
                             head_stack(gather(k_fac, half_groups(c, half, False))))
            if 2 * half < HGRN_CHUNK:
                scores = jnp.where((pair_t // half) == (pair_s // half), scores, 0.0)
            probs[c, half] = scores.astype(BF16)
            n_small += 1
            if n_small % 4 == 0:
                fill()
    intra = [[None] * groups_per_chunk for _ in range(n_chunks)]
    for c in range(n_chunks):
        for half, _, _ in levels:
            o_l = _dot(probs[c, half], head_stack(gather(v, half_groups(c, half, False))))
            for i, g in enumerate(half_groups(c, half, True)):
                part = o_l[i * SUBLANES:(i + 1) * SUBLANES]
                g_in = g - c * groups_per_chunk
                intra[c][g_in] = part if intra[c][g_in] is None else intra[c][g_in] + part
            n_small += 1
            if n_small % 6 == 0:
                fill()

    merged = merge(merged, 3, y_d)

    assert n_chunks == SUBLANES
    st = st_ref[...]
    o_parts = []
    o_same = None
    decay = None
    f_d = f
    for d, sl in enumerate(chunk_rows):
        o_c = _dot_nt(qe_b[sl], st.astype(BF16))
        o_parts.append(o_c + jnp.concatenate(
            [jnp.zeros((SUBLANES, W), F32) if part is None else part for part in intra[d]], axis=0))
        st = st * decay_chunk[d] + jnp.where(same_head, updates[d], 0.0)
        if d == 0:
            w_d, v_d = c_q * kk, v
        else:
            decay = f if d == 1 else decay * f_d
            f_d = _group_roll(f, d)
            w_d = jnp.where(sub >= d, c_q * (1.0 - f_d) * decay, 0.0)
            v_d = _group_roll(v, d)
        term = _dot(w_d.astype(BF16), head_ones) * v_d
        o_same = term if o_same is None else o_same + term
        fill()
    st_ref[...] = st
    o = jnp.concatenate(o_parts, axis=0) + o_same

    ms = _dot((o * o).astype(BF16), head_ones) * (1.0 / HGRN_DK)
    y_c = o * lax.rsqrt(ms + EPS) * layer_row(hnorm_ref) * jax.nn.silu(proj(6))
    merged = merge(merged, 2, y_c)

    merged_b = merged.astype(BF16)
    for rows in (slice(0, tt // 2), slice(tt // 2, tt)):
        mix = _dot(merged_b[rows], wout_ref[...])
        out_ref[0, rows, :] = h_ref[0, rows, :] + _rms(mix, layer_row(gpost_ref))


def _mlp_kernel(layer, n_cast, h_ref, gpre_ref, gpost_ref, wup_ref, wdown_ref, *rest):
    cast_srcs, out_ref, cast_dsts = rest[:n_cast], rest[n_cast], rest[n_cast + 1:]
    _cast_slabs(cast_srcs + cast_dsts)
    gain_pre, gain_post = gpre_ref[layer:layer + 1, :], gpost_ref[layer:layer + 1, :]
    n_rows = h_ref.shape[0]
    for i in range(MLP_ROW_BLOCKS):
        rows = slice(i * n_rows // MLP_ROW_BLOCKS, (i + 1) * n_rows // MLP_ROW_BLOCKS)
        h_in = h_ref[rows, :]
        u = _rms(h_in, gain_pre).astype(BF16)
        m = jnp.zeros(h_in.shape, F32)
        for j in range(D_FF // FF_CHUNK):
            cols = slice(j * FF_CHUNK, (j + 1) * FF_CHUNK)
            hid = jnp.square(jnp.maximum(_dot(u, wup_ref[:, cols]), 0.0))
            m = m + _dot(hid.astype(BF16), wdown_ref[cols, :])
        out_ref[rows, :] = h_in + _rms(m, gain_post)


def _whole(arr):
    nd = arr.ndim
    return pl.BlockSpec(arr.shape, lambda *_: (0,) * nd, pipeline_mode=pl.Buffered(1))


def _cast_specs(sources, layer, n_steps, step_of):
    in_specs, out_specs, out_shapes = [], [], []
    for src in sources:
        _, rows, cols = src.shape
        slab = rows // n_steps
        in_specs.append(pl.BlockSpec((None, slab, cols), lambda *g: (layer, step_of(*g), 0)))
        out_specs.append(pl.BlockSpec((slab, cols), lambda *g: (step_of(*g), 0)))
        out_shapes.append(jax.ShapeDtypeStruct((rows, cols), BF16))
    return in_specs, out_specs, out_shapes


def _block_diag(blocks):
    depth, n, bi, bo = blocks.shape
    eye = jnp.eye(n, dtype=blocks.dtype)
    return jnp.einsum("lnio,nm->lnimo", blocks, eye).reshape(depth, n * bi, n * bo).astype(BF16)


def _mixer_layer(h, layer, params, weights, cast_sources):
    bsz, seq, _ = h.shape
    n_t = seq // TIME_TILE
    w_in_b, w_branch_b, w_out_b = weights
    operands = [
        h,
        params["norm_mix_pre"], params["norm_mix_post"],
        w_in_b,
        params["lru_conv_w"], params["lru_conv_b"],
        params["lru_w_a_dense"], params["lru_b_a"],
        params["lru_w_x_dense"], params["lru_b_x"],
        params["lru_lambda"],
        params["pool_w_dense"], params["pool_scale"],
        params["hgrn_lower_bound"],
        params["hgrn_norm_heads"],
        params["sconv_w"],
        w_branch_b,
        w_out_b,
    ]
    cast_in, cast_out, cast_shapes = _cast_specs(cast_sources, layer, bsz * n_t, lambda bi, ti: bi * n_t + ti)
    in_specs = [pl.BlockSpec((1, TIME_TILE, D_MODEL), lambda bi, ti: (bi, ti, 0))]
    in_specs += [_whole(op) for op in operands[1:]] + cast_in
    h_out, *cast = pl.pallas_call(
        functools.partial(_mixer_kernel, layer, len(cast_sources)),
        grid=(bsz, n_t),
        in_specs=in_specs,
        out_specs=[pl.BlockSpec((1, TIME_TILE, D_MODEL), lambda bi, ti: (bi, ti, 0))] + cast_out,
        out_shape=[jax.ShapeDtypeStruct(h.shape, h.dtype)] + cast_shapes,
        scratch_shapes=[
            pltpu.VMEM((3, TAIL, W), F32),
            pltpu.VMEM((8, W), F32),
            pltpu.VMEM((W, W), F32),
        ],
        compiler_params=pltpu.CompilerParams(
            dimension_semantics=("parallel", "arbitrary"),
            vmem_limit_bytes=VMEM_LIMIT_BYTES),
        name=f"mixer_l{layer}",
    )(*operands, *cast_sources)
    return h_out, cast


def _mlp_layer(h, layer, params, weights, cast_sources, cast_layer):
    bsz, seq, d = h.shape
    h2 = h.reshape(bsz * seq, d)
    n_steps = bsz * seq // MLP_TILE
    operands = [h2, params["norm_mlp_pre"], params["norm_mlp_post"], *weights]
    cast_in, cast_out, cast_shapes = _cast_specs(cast_sources, cast_layer, n_steps, lambda i: i)
    in_specs = [pl.BlockSpec((MLP_TILE, d), lambda i: (i, 0))] + [_whole(op) for op in operands[1:]] + cast_in
    out, *cast = pl.pallas_call(
        functools.partial(_mlp_kernel, layer, len(cast_sources)),
        grid=(n_steps,),
        in_specs=in_specs,
        out_specs=[pl.BlockSpec((MLP_TILE, d), lambda i: (i, 0))] + cast_out,
        out_shape=[jax.ShapeDtypeStruct(h2.shape, h2.dtype)] + cast_shapes,
        compiler_params=pltpu.CompilerParams(
            dimension_semantics=("parallel",),
            vmem_limit_bytes=VMEM_LIMIT_BYTES),
        name=f"mlp_l{layer}",
    )(*operands, *cast_sources)
    return out.reshape(bsz, seq, d), cast


def kernel(x, norm_mix_pre, norm_mix_post, norm_mlp_pre, norm_mlp_post, w_in, lru_conv_w, lru_conv_b,
           lru_w_a, lru_b_a, lru_w_x, lru_b_x, lru_lambda, pool_w, pool_scale, hgrn_lower_bound,
           hgrn_norm, sconv_w, w_branch, w_out, w_up, w_down):
    params = dict(
        norm_mix_pre=norm_mix_pre, norm_mix_post=norm_mix_post, norm_mlp_pre=norm_mlp_pre,
        norm_mlp_post=norm_mlp_post, lru_conv_w=lru_conv_w, lru_conv_b=lru_conv_b,
        lru_w_a_dense=_block_diag(lru_w_a), lru_b_a=lru_b_a, lru_w_x_dense=_block_diag(lru_w_x),
        lru_b_x=lru_b_x, lru_lambda=lru_lambda, pool_w_dense=_block_diag(pool_w), pool_scale=pool_scale,
        hgrn_lower_bound=hgrn_lower_bound, hgrn_norm_heads=jnp.tile(hgrn_norm, (1, HGRN_HEADS)),
        sconv_w=sconv_w)
    depth = w_in.shape[0]
    w_branch2 = w_branch.reshape(depth, N_BRANCHES * W, D_MODEL)
    mixer_sources = (w_in, w_branch2, w_out)
    mlp_sources = (w_up, w_down)
    mixer_weights = [w[0].astype(BF16) for w in mixer_sources]
    h = x
    for layer in range(depth):
        h, mlp_weights = _mixer_layer(h, layer, params, mixer_weights, mlp_sources)
        last = layer + 1 == depth
        h, mixer_weights = _mlp_layer(h, layer, params, mlp_weights,
                                      () if last else mixer_sources, layer + 1)
    return h
```

```python
import functools

import jax
import jax.numpy as jnp
from jax import lax
from jax.experimental import pallas as pl
from jax.experimental.pallas import tpu as pltpu

D_MODEL = 1024
N_BRANCHES = 4
W = D_MODEL // N_BRANCHES
LRU_CONV = 4
LRU_C = 8.0
POOL_WINDOWS = (2, 4, 8, 16)
POOL_GROUP = W // len(POOL_WINDOWS)
HGRN_HEADS = 4
HGRN_DK = W // HGRN_HEADS
SCONV_WIDTH = 3
D_FF = 4 * D_MODEL
N_MIX_SLOTS = 10
MIX_WIDTH = N_MIX_SLOTS * W
EPS = 1e-6

TIME_TILE = 512
HGRN_CHUNK = 64
SUBLANES = 8
EDGE_BLOCKS = 4
TAIL = 16
MLP_TILE = 1024
MLP_ROW_BLOCKS = 2
FF_CHUNK = 2048
VMEM_LIMIT_BYTES = 56 * 1024 * 1024

F32 = jnp.float32
BF16 = jnp.bfloat16


def _dot(a, b):
    return jnp.dot(a, b, preferred_element_type=F32)


def _dot_nt(a, b):
    return lax.dot_general(a, b, (((1,), (1,)), ((), ())), preferred_element_type=F32)


def _dot_tn(a, b):
    return lax.dot_general(a, b, (((0,), (0,)), ((), ())), preferred_element_type=F32)


def _rms(x, gain):
    return x * lax.rsqrt(jnp.mean(x * x, axis=-1, keepdims=True) + EPS) * gain


def _group_roll(x, s):
    n, w = x.shape
    return pltpu.roll(x.reshape(n // SUBLANES, SUBLANES, w), s, 1).reshape(n, w)


def _shift_rows(ext, s, n):
    if s == 0:
        return ext[TAIL:TAIL + n]
    return pltpu.roll(ext, s, 0)[TAIL:TAIL + n]


def _cast_slabs(refs):
    n = len(refs) // 2
    for src_ref, dst_ref in zip(refs[:n], refs[n:]):
        dst_ref[...] = src_ref[...].astype(BF16)


def _mixer_kernel(lb_layer, n_cast, h_ref, gpre_ref, gpost_ref, win_ref, convw_ref, convb_ref, wa_ref,
                  ba_ref, wx_ref, bx_ref, lam_ref, poolw_ref, pools_ref, lbraw_ref, hnorm_ref, sconvw_ref,
                  wbr_ref, wout_ref, *rest):
    cast_srcs, out_ref, cast_dsts = rest[:n_cast], rest[n_cast], rest[n_cast + 1:2 * n_cast + 1]
    tail_ref, hlru_ref, st_ref = rest[2 * n_cast + 1:]
    _cast_slabs(cast_srcs + cast_dsts)
    tt = h_ref.shape[1]
    n_chunks = tt // HGRN_CHUNK

    @pl.when(pl.program_id(1) == 0)
    def _():
        tail_ref[...] = jnp.zeros_like(tail_ref)
        hlru_ref[...] = jnp.zeros_like(hlru_ref)
        st_ref[...] = jnp.zeros_like(st_ref)

    row_blocks = [slice(i * tt // EDGE_BLOCKS, (i + 1) * tt // EDGE_BLOCKS) for i in range(EDGE_BLOCKS)]
    layer_row = lambda ref: ref[lb_layer:lb_layer + 1, :]
    u_blocks = [_rms(h_ref[0, rows, :], layer_row(gpre_ref)).astype(BF16) for rows in row_blocks]
    u = jnp.concatenate(u_blocks, axis=0)

    row = lax.broadcasted_iota(jnp.int32, (tt, W), 0)
    lane = lax.broadcasted_iota(jnp.int32, (tt, W), 1)
    sub = row & (SUBLANES - 1)
    n_groups = tt // SUBLANES

    gate_cols = D_MODEL // W
    gate_keys = lambda k: [("gate", k, j) for j in range(gate_cols)]
    pending = (gate_keys(0) + [("slot", 1)] + gate_keys(1) + gate_keys(2)
               + [("slot", s) for s in (6, 7, 8, 9)] + gate_keys(3))
    early = {}

    def issue(key):
        pending.remove(key)
        if key[0] == "slot":
            early[key] = _dot(u, win_ref[:, key[1] * W:(key[1] + 1) * W])
        else:
            lo = MIX_WIDTH + key[1] * D_MODEL + key[2] * W
            early[key] = jax.nn.sigmoid(_dot(u, win_ref[:, lo:lo + W])).astype(BF16)

    def fill():
        if pending:
            issue(pending[0])

    def take(key):
        if key in pending:
            issue(key)
        return early.pop(key)

    def proj(slot):
        if slot == 0:
            w_slot = win_ref[:, 0:W]
            return jnp.concatenate([_dot(u_blk, w_slot) for u_blk in u_blocks], axis=0)
        if ("slot", slot) in pending or ("slot", slot) in early:
            return take(("slot", slot))
        return _dot(u, win_ref[:, slot * W:(slot + 1) * W])

    def merge(acc, k, y_k):
        gate = jnp.concatenate([take(key) for key in gate_keys(k)], axis=1)
        return acc + gate.astype(F32) * _dot(y_k.astype(BF16), wbr_ref[k * W:(k + 1) * W, :])

    a_x = proj(0)
    ext = jnp.concatenate([tail_ref[0], a_x], axis=0)
    tail_ref[0] = a_x[tt - TAIL:]
    xc = layer_row(convb_ref) + sum(convw_ref[lb_layer, k:k + 1, :] * _shift_rows(ext, LRU_CONV - 1 - k, tt)
                                    for k in range(LRU_CONV))
    fill()
    xc_b = xc.astype(BF16)
    r = jax.nn.sigmoid(_dot(xc_b, wa_ref[lb_layer]) + layer_row(ba_ref))
    i_gate = jax.nn.sigmoid(_dot(xc_b, wx_ref[lb_layer]) + layer_row(bx_ref))
    log_a = (-LRU_C) * r * jax.nn.softplus(-layer_row(lam_ref))
    a = jnp.exp(log_a)
    mult = jnp.sqrt(jnp.tanh(-log_a) * (1.0 + a * a))
    uu = mult * i_gate * xc
    fill()
    s = 1
    while s < SUBLANES:
        keep = sub >= s
        a_sh = jnp.where(keep, _group_roll(a, s), 1.0)
        u_sh = jnp.where(keep, _group_roll(uu, s), 0.0)
        uu = uu + a * u_sh
        a = a * a_sh
        s *= 2
    fill()
    a_g = a.reshape(n_groups, SUBLANES, W)
    u_g = uu.reshape(n_groups, SUBLANES, W)
    h_prev = hlru_ref[0:1, :]
    h_groups = []
    for g in range(n_groups):
        h_g = u_g[g] + a_g[g] * h_prev
        h_groups.append(h_g)
        h_prev = h_g[SUBLANES - 1:SUBLANES, :]
    hlru_ref[0:1, :] = h_prev
    h_seq = jnp.concatenate(h_groups, axis=0)
    y_a = h_seq * jax.nn.gelu(proj(1))
    merged = merge(jnp.zeros((tt, D_MODEL), F32), 0, y_a)

    p_x = proj(2)
    ext = jnp.concatenate([tail_ref[1], p_x], axis=0)
    tail_ref[1] = p_x[tt - TAIL:]
    group = lax.shift_right_logical(lane, POOL_GROUP.bit_length() - 1)
    win_sum = ext
    mean = jnp.zeros((tt, W), F32)
    pos = (pl.program_id(1) * tt + row + 1).astype(F32)
    for g, win in enumerate(POOL_WINDOWS):
        win_sum = win_sum + pltpu.roll(win_sum, win // 2, 0)
        mean = jnp.where(group == g, win_sum[TAIL:] / jnp.minimum(pos, float(win)), mean)
    fill()
    y_b = _dot((mean - p_x).astype(BF16), poolw_ref[lb_layer]) * layer_row(pools_ref)
    merged = merge(merged, 1, y_b)

    s_b = proj(7)
    z = proj(8) * proj(9)
    ext = jnp.concatenate([tail_ref[2], z], axis=0)
    tail_ref[2] = z[tt - TAIL:]
    y_d = s_b * sum(sconvw_ref[lb_layer, k:k + 1, :] * _shift_rows(ext, SCONV_WIDTH - 1 - k, tt)
                    for k in range(SCONV_WIDTH))

    c_q = proj(3)
    c_f = proj(4)
    v = proj(5)
    lb_sm = jax.nn.softmax(lbraw_ref[...], axis=0)
    lb = jnp.sum(lb_sm[1:lb_layer + 1], axis=0, keepdims=True) if lb_layer else jnp.zeros((1, W), F32)
    f = lb + (1.0 - lb) * jax.nn.sigmoid(c_f)
    lf = jnp.log(f)
    kk = 1.0 - f
    b = lf
    s = 1
    while s < SUBLANES:
        b = b + jnp.where(sub >= s, _group_roll(b, s), 0.0)
        s *= 2
    b_g = b.reshape(n_groups, SUBLANES, W)
    b_parts = []
    for g in range(n_groups):
        if g % (HGRN_CHUNK // SUBLANES) == 0:
            b_parts.append(b_g[g])
        else:
            b_parts.append(b_g[g] + b_parts[-1][SUBLANES - 1:SUBLANES, :])
    b = jnp.concatenate(b_parts, axis=0)
    fill()
    b3 = b.reshape(n_chunks, HGRN_CHUNK, W)
    b_last = b3[:, HGRN_CHUNK - 1:HGRN_CHUNK, :]
    qe_b = (c_q * jnp.exp(b)).astype(BF16)
    kd_b = (kk.reshape(n_chunks, HGRN_CHUNK, W) * jnp.exp(b_last - b3)).reshape(tt, W).astype(BF16)
    decay_chunk = jnp.exp(b_last)
    v_b = v.astype(BF16)

    sq_r = lax.broadcasted_iota(jnp.int32, (W, W), 0)
    sq_c = lax.broadcasted_iota(jnp.int32, (W, W), 1)
    same_head = (sq_r ^ sq_c) < HGRN_DK
    head_ones = jnp.where(same_head, 1.0, 0.0).astype(BF16)

    half_rows = HGRN_CHUNK // 2
    groups_per_chunk = HGRN_CHUNK // SUBLANES
    levels = []
    half = SUBLANES
    while half < HGRN_CHUNK:
        blk = 2 * half
        bb = b.reshape(tt // blk, blk, W)
        delta = (bb - bb[:, half - 1:half, :]).reshape(tt, W)
        levels.append((half, c_q * jnp.exp(jnp.minimum(delta, 0.0)), kk * jnp.exp(jnp.minimum(-delta, 0.0))))
        half = blk

    def half_groups(c, half, second):
        return [c * groups_per_chunk + g for g in range(groups_per_chunk)
                if ((g * SUBLANES) & half != 0) == second]

    def gather(x, groups):
        return jnp.concatenate([x[g * SUBLANES:(g + 1) * SUBLANES] for g in groups], axis=0).astype(BF16)

    stack_r = lax.broadcasted_iota(jnp.int32, (HGRN_HEADS * half_rows, W), 0)
    stack_c = lax.broadcasted_iota(jnp.int32, (HGRN_HEADS * half_rows, W), 1)
    stack_same_head = (stack_r // half_rows) == (stack_c // HGRN_DK)

    def head_stack(x_rows):
        return jnp.where(stack_same_head, jnp.concatenate([x_rows] * HGRN_HEADS, axis=0), 0.0)

    pair_t = lax.broadcasted_iota(jnp.int32, (half_rows, HGRN_HEADS * half_rows), 0)
    pair_s = lax.broadcasted_iota(jnp.int32, (half_rows, HGRN_HEADS * half_rows), 1) % half_rows

    chunk_rows = [slice(c * HGRN_CHUNK, (c + 1) * HGRN_CHUNK) for c in range(n_chunks)]
    updates = []
    for c, sl in enumerate(chunk_rows):
        updates.append(_dot_tn(v_b[sl], kd_b[sl]))
        if c % 4 == 3:
            fill()
    n_small = 0
    probs = {}
    for c in range(n_chunks):
        for half, q_fac, k_fac in levels:
            scores = _dot_nt(gather(q_fac, half_groups(c, half, True)),
                             head_stack(gather(k_fac, half_groups(c, half, False))))
            if 2 * half < HGRN_CHUNK:
                scores = jnp.where((pair_t // half) == (pair_s // half), scores, 0.0)
            probs[c, half] = scores.astype(BF16)
            n_small += 1
            if n_small % 4 == 0:
                fill()
    intra = [[None] * groups_per_chunk for _ in range(n_chunks)]
    for c in range(n_chunks):
        for half, _, _ in levels:
            o_l = _dot(probs[c, half], head_stack(gather(v, half_groups(c, half, False))))
            for i, g in enumerate(half_groups(c, half, True)):
                part = o_l[i * SUBLANES:(i + 1) * SUBLANES]
                g_in = g - c * groups_per_chunk
                intra[c][g_in] = part if intra[c][g_in] is None else intra[c][g_in] + part
            n_small += 1
            if n_small % 6 == 0:
                fill()

    merged = merge(merged, 3, y_d)

    assert n_chunks == SUBLANES
    st = st_ref[...]
    o_parts = []
    o_same = None
    decay = None
    f_d = f
    for d, sl in enumerate(chunk_rows):
        o_c = _dot_nt(qe_b[sl], st.astype(BF16))
        o_parts.append(o_c + jnp.concatenate(
            [jnp.zeros((SUBLANES, W), F32) if part is None else part for part in intra[d]], axis=0))
        st = st * decay_chunk[d] + jnp.where(same_head, updates[d], 0.0)
        if d == 0:
            w_d, v_d = c_q * kk, v
        else:
            decay = f if d == 1 else decay * f_d
            f_d = _group_roll(f, d)
            w_d = jnp.where(sub >= d, c_q * (1.0 - f_d) * decay, 0.0)
            v_d = _group_roll(v, d)
        term = _dot(w_d.astype(BF16), head_ones) * v_d
        o_same = term if o_same is None else o_same + term
        fill()
    st_ref[...] = st
    o = jnp.concatenate(o_parts, axis=0) + o_same

    ms = _dot((o * o).astype(BF16), head_ones) * (1.0 / HGRN_DK)
    y_c = o * lax.rsqrt(ms + EPS) * layer_row(hnorm_ref) * jax.nn.silu(proj(6))
    merged = merge(merged, 2, y_c)

    merged_b = merged.astype(BF16)
    for rows in (slice(0, tt // 2), slice(tt // 2, tt)):
        mix = _dot(merged_b[rows], wout_ref[...])
        out_ref[0, rows, :] = h_ref[0, rows, :] + _rms(mix, layer_row(gpost_ref))


def _mlp_kernel(layer, n_cast, h_ref, gpre_ref, gpost_ref, wup_ref, wdown_ref, *rest):
    cast_srcs, out_ref, cast_dsts = rest[:n_cast], rest[n_cast], rest[n_cast + 1:]
    _cast_slabs(cast_srcs + cast_dsts)
    gain_pre, gain_post = gpre_ref[layer:layer + 1, :], gpost_ref[layer:layer + 1, :]
    n_rows = h_ref.shape[0]
    for i in range(MLP_ROW_BLOCKS):
        rows = slice(i * n_rows // MLP_ROW_BLOCKS, (i + 1) * n_rows // MLP_ROW_BLOCKS)
        h_in = h_ref[rows, :]
        u = _rms(h_in, gain_pre).astype(BF16)
        m = jnp.zeros(h_in.shape, F32)
        for j in range(D_FF // FF_CHUNK):
            cols = slice(j * FF_CHUNK, (j + 1) * FF_CHUNK)
            hid = jnp.square(jnp.maximum(_dot(u, wup_ref[:, cols]), 0.0))
            m = m + _dot(hid.astype(BF16), wdown_ref[cols, :])
        out_ref[rows, :] = h_in + _rms(m, gain_post)


def _whole(arr):
    nd = arr.ndim
    return pl.BlockSpec(arr.shape, lambda *_: (0,) * nd, pipeline_mode=pl.Buffered(1))


def _cast_specs(sources, layer, n_steps, step_of):
    in_specs, out_specs, out_shapes = [], [], []
    for src in sources:
        _, rows, cols = src.shape
        slab = rows // n_steps
        in_specs.append(pl.BlockSpec((None, slab, cols), lambda *g: (layer, step_of(*g), 0)))
        out_specs.append(pl.BlockSpec((slab, cols), lambda *g: (step_of(*g), 0)))
        out_shapes.append(jax.ShapeDtypeStruct((rows, cols), BF16))
    return in_specs, out_specs, out_shapes


def _block_diag(blocks):
    depth, n, bi, bo = blocks.shape
    eye = jnp.eye(n, dtype=blocks.dtype)
    return jnp.einsum("lnio,nm->lnimo", blocks, eye).reshape(depth, n * bi, n * bo).astype(BF16)


def _mixer_layer(h, layer, params, weights, cast_sources):
    bsz, seq, _ = h.shape
    n_t = seq // TIME_TILE
    w_in_b, w_branch_b, w_out_b = weights
    operands = [
        h,
        params["norm_mix_pre"], params["norm_mix_post"],
        w_in_b,
        params["lru_conv_w"], params["lru_conv_b"],
        params["lru_w_a_dense"], params["lru_b_a"],
        params["lru_w_x_dense"], params["lru_b_x"],
        params["lru_lambda"],
        params["pool_w_dense"], params["pool_scale"],
        params["hgrn_lower_bound"],
        params["hgrn_norm_heads"],
        params["sconv_w"],
        w_branch_b,
        w_out_b,
    ]
    cast_in, cast_out, cast_shapes = _cast_specs(cast_sources, layer, bsz * n_t, lambda bi, ti: bi * n_t + ti)
    in_specs = [pl.BlockSpec((1, TIME_TILE, D_MODEL), lambda bi, ti: (bi, ti, 0))]
    in_specs += [_whole(op) for op in operands[1:]] + cast_in
    h_out, *cast = pl.pallas_call(
        functools.partial(_mixer_kernel, layer, len(cast_sources)),
        grid=(bsz, n_t),
        in_specs=in_specs,
        out_specs=[pl.BlockSpec((1, TIME_TILE, D_MODEL), lambda bi, ti: (bi, ti, 0))] + cast_out,
        out_shape=[jax.ShapeDtypeStruct(h.shape, h.dtype)] + cast_shapes,
        scratch_shapes=[
            pltpu.VMEM((3, TAIL, W), F32),
            pltpu.VMEM((8, W), F32),
            pltpu.VMEM((W, W), F32),
        ],
        compiler_params=pltpu.CompilerParams(
            dimension_semantics=("parallel", "arbitrary"),
            vmem_limit_bytes=VMEM_LIMIT_BYTES),
        name=f"mixer_l{layer}",
    )(*operands, *cast_sources)
    return h_out, cast


def _mlp_layer(h, layer, params, weights, cast_sources, cast_layer):
    bsz, seq, d = h.shape
    h2 = h.reshape(bsz * seq, d)
    n_steps = bsz * seq // MLP_TILE
    operands = [h2, params["norm_mlp_pre"], params["norm_mlp_post"], *weights]
    cast_in, cast_out, cast_shapes = _cast_specs(cast_sources, cast_layer, n_steps, lambda i: i)
    in_specs = [pl.BlockSpec((MLP_TILE, d), lambda i: (i, 0))] + [_whole(op) for op in operands[1:]] + cast_in
    out, *cast = pl.pallas_call(
        functools.partial(_mlp_kernel, layer, len(cast_sources)),
        grid=(n_steps,),
        in_specs=in_specs,
        out_specs=[pl.BlockSpec((MLP_TILE, d), lambda i: (i, 0))] + cast_out,
        out_shape=[jax.ShapeDtypeStruct(h2.shape, h2.dtype)] + cast_shapes,
        compiler_params=pltpu.CompilerParams(
            dimension_semantics=("parallel",),
            vmem_limit_bytes=VMEM_LIMIT_BYTES),
        name=f"mlp_l{layer}",
    )(*operands, *cast_sources)
    return out.reshape(bsz, seq, d), cast


def kernel(x, norm_mix_pre, norm_mix_post, norm_mlp_pre, norm_mlp_post, w_in, lru_conv_w, lru_conv_b,
           lru_w_a, lru_b_a, lru_w_x, lru_b_x, lru_lambda, pool_w, pool_scale, hgrn_lower_bound,
           hgrn_norm, sconv_w, w_branch, w_out, w_up, w_down):
    params = dict(
        norm_mix_pre=norm_mix_pre, norm_mix_post=norm_mix_post, norm_mlp_pre=norm_mlp_pre,
        norm_mlp_post=norm_mlp_post, lru_conv_w=lru_conv_w, lru_conv_b=lru_conv_b,
        lru_w_a_dense=_block_diag(lru_w_a), lru_b_a=lru_b_a, lru_w_x_dense=_block_diag(lru_w_x),
        lru_b_x=lru_b_x, lru_lambda=lru_lambda, pool_w_dense=_block_diag(pool_w), pool_scale=pool_scale,
        hgrn_lower_bound=hgrn_lower_bound, hgrn_norm_heads=jnp.tile(hgrn_norm, (1, HGRN_HEADS)),
        sconv_w=sconv_w)
    depth = w_in.shape[0]
    w_branch2 = w_branch.reshape(depth, N_BRANCHES * W, D_MODEL)
    mixer_sources = (w_in, w_branch2, w_out)
    mlp_sources = (w_up, w_down)
    mixer_weights = [w[0].astype(BF16) for w in mixer_sources]
    h = x
    for layer in range(depth):
        h, mlp_weights = _mixer_layer(h, layer, params, mixer_weights, mlp_sources)
        last = layer + 1 == depth
        h, mixer_weights = _mlp_layer(h, layer, params, mlp_weights,
                                      () if last else mixer_sources, layer + 1)
    return h
```
